```python
import jax, jax.numpy as jnp
from jax import lax
import numpy as np

D_MODEL = 2048
BATCH = 8
SEQ = 2048
DEPTH = 2

MEM_LEN = 256
NORM_EPS = 1e-6
ROPE_THETA = 500000.0

D_FF = 5632

SSD_EXPAND = 2
SSD_INNER = SSD_EXPAND * D_MODEL
SSD_HEAD_DIM = 64
SSD_HEADS = SSD_INNER // SSD_HEAD_DIM
SSD_GROUPS = 8
SSD_HEADS_PER_GROUP = SSD_HEADS // SSD_GROUPS
SSD_STATE = 128
SSD_CONV = 4
SSD_CHUNK = 128
SSD_CONV_CH = SSD_INNER + 2 * SSD_GROUPS * SSD_STATE

DSA_HEADS = 16
DSA_WIDTH = D_MODEL
DSA_HEAD_DIM = DSA_WIDTH // DSA_HEADS
DSA_ROPE = DSA_HEAD_DIM // 4
IDX_HEADS = 16
IDX_DIM = 64
IDX_ROPE = IDX_DIM // 4
DSA_TOPK_MAX = 256
Q_BLOCK = 128

XA_HEADS = 4
XA_WIDTH = D_MODEL
XA_HEAD_DIM = XA_WIDTH // XA_HEADS

N_BRANCH = 3

IN_SPLITS = (SSD_INNER, SSD_CONV_CH, SSD_HEADS,
             DSA_WIDTH, DSA_HEAD_DIM, DSA_HEAD_DIM,
             IDX_HEADS * IDX_DIM, IDX_DIM, IDX_HEADS,
             XA_WIDTH, N_BRANCH * D_MODEL)
IN_WIDTH = int(sum(IN_SPLITS))
IN_OFFSETS = tuple(int(o) for o in np.cumsum(IN_SPLITS)[:-1].tolist())

kernel_name = 'hybrid_ssd_dsa_memxattn_macaron'


def rms_norm(x, g):
    xf = x.astype(jnp.float32)
    y = xf * lax.rsqrt(jnp.mean(xf * xf, axis=-1, keepdims=True) + NORM_EPS)
    return (y * g.astype(jnp.float32)).astype(x.dtype)


def swiglu(h, w_in, w_out):
    gate, up = jnp.split(h @ w_in, 2, axis=-1)
    return (jax.nn.silu(gate) * up) @ w_out


def rope_tables(positions, rot_dim):
    inv = jnp.power(ROPE_THETA, -jnp.arange(0, rot_dim, 2, dtype=jnp.float32) / rot_dim)
    ang = positions.astype(jnp.float32)[..., None] * inv
    return jnp.cos(ang), jnp.sin(ang)


def apply_partial_rope(x, cos, sin):
    half = cos.shape[-1]
    rot = 2 * half
    extra = x.ndim - cos.ndim
    shp = cos.shape[:2] + (1,) * extra + (half,)
    c = cos.reshape(shp)
    s = sin.reshape(shp)
    xr = x[..., :rot].astype(jnp.float32)
    x1, x2 = xr[..., :half], xr[..., half:]
    out = jnp.concatenate([x1 * c - x2 * s, x2 * c + x1 * s], axis=-1).astype(x.dtype)
    return jnp.concatenate([out, x[..., rot:]], axis=-1)


def causal_depthwise_conv(x, w, b):
    ch = x.shape[-1]
    y = lax.conv_general_dilated(x, w[:, None, :].astype(x.dtype), window_strides=(1,),
                                 padding=[(SSD_CONV - 1, 0)],
                                 dimension_numbers=('NWC', 'WIO', 'NWC'),
                                 feature_group_count=ch)
    return y + b.astype(x.dtype)


def ssd_mixer(z, xbc, dt_raw, conv_w, conv_b, dt_bias, a_log, d_skip, norm_g):
    f32 = jnp.float32
    bsz, seq, _ = z.shape
    G, R, P, N, Q = SSD_GROUPS, SSD_HEADS_PER_GROUP, SSD_HEAD_DIM, SSD_STATE, SSD_CHUNK
    nc = seq // Q
    xbc = jax.nn.silu(causal_depthwise_conv(xbc, conv_w, conv_b))
    xs, bm, cm = jnp.split(xbc, [SSD_INNER, SSD_INNER + G * N], axis=-1)
    xs = xs.reshape(bsz, nc, Q, G, R, P).astype(f32)
    bm = bm.reshape(bsz, nc, Q, G, N).astype(f32)
    cm = cm.reshape(bsz, nc, Q, G, N).astype(f32)
    dt = jax.nn.softplus(dt_raw.astype(f32) + dt_bias.astype(f32)).reshape(bsz, nc, Q, G, R)
    a = -jnp.exp(a_log.astype(f32)).reshape(G, R)
    a_cs = jnp.cumsum(dt * a, axis=2)
    xdt = xs * dt[..., None]
    causal = jnp.tril(jnp.ones((Q, Q), dtype=bool))
    seg = a_cs[:, :, :, None] - a_cs[:, :, None, :]
    decay = jnp.exp(jnp.where(causal[None, None, :, :, None, None], seg, -jnp.inf))
    cb = jnp.einsum('bclgn,bcsgn->bclsg', cm, bm)
    y_diag = jnp.einsum('bclsgr,bcsgrp->bclgrp', cb[..., None] * decay, xdt)
    decay_to_end = jnp.exp(a_cs[:, :, -1:] - a_cs)
    states = jnp.einsum('bcsgn,bcsgrp->bcgrpn', bm, xdt * decay_to_end[..., None])
    chunk_decay = jnp.exp(a_cs[:, :, -1])

    def step(h, inp):
        st, dc = inp
        return h * dc[..., None, None] + st, h

    h0 = jnp.zeros((bsz, G, R, P, N), f32)
    _, prev = lax.scan(step, h0, (jnp.moveaxis(states, 1, 0), jnp.moveaxis(chunk_decay, 1, 0)))
    prev = jnp.moveaxis(prev, 0, 1)
    y_off = jnp.einsum('bclgn,bcgrpn->bclgrp', cm, prev) * jnp.exp(a_cs)[..., None]
    y = y_diag + y_off + xs * d_skip.astype(f32).reshape(G, R)[..., None]
    y = y.reshape(bsz, seq, SSD_INNER) * jax.nn.silu(z.astype(f32))
    return rms_norm(y, norm_g).astype(z.dtype)


def dsa_mixer(q, k, v, qi, ki, wi, cos_a, sin_a, cos_i, sin_i):
    f32 = jnp.float32
    bsz, seq, _ = q.shape
    q = apply_partial_rope(q.reshape(bsz, seq, DSA_HEADS, DSA_HEAD_DIM), cos_a, sin_a)
    k = apply_partial_rope(k, cos_a, sin_a)
    qi = apply_partial_rope(qi.reshape(bsz, seq, IDX_HEADS, IDX_DIM), cos_i, sin_i)
    ki = apply_partial_rope(ki, cos_i, sin_i)
    wi = wi.astype(f32) * IDX_HEADS ** -0.5
    topk = min(DSA_TOPK_MAX, seq // 4)
    nb = seq // Q_BLOCK
    key_pos = jnp.arange(seq)

    def to_blocks(t):
        return jnp.moveaxis(t.reshape((bsz, nb, Q_BLOCK) + t.shape[2:]), 1, 0)

    def block(args):
        q_b, qi_b, w_b, q_pos = args
        s = jnp.einsum('bqhd,bkd->bqhk', qi_b, ki).astype(f32) * IDX_DIM ** -0.5
        score = jnp.einsum('bqhk,bqh->bqk', jax.nn.relu(s), w_b)
        score = jnp.where(key_pos[None, None, :] <= q_pos[None, :, None], score, -jnp.inf)
        _, idx = lax.top_k(score, topk)
        k_sel = jax.vmap(lambda kk, ii: kk[ii])(k, idx)
        v_sel = jax.vmap(lambda vv, ii: vv[ii])(v, idx)
        logits = jnp.einsum('bqhd,bqkd->bqhk', q_b, k_sel).astype(f32) * DSA_HEAD_DIM ** -0.5
        valid = (idx <= q_pos[None, :, None])[:, :, None, :]
        p = jax.nn.softmax(jnp.where(valid, logits, -jnp.inf), axis=-1)
        return jnp.einsum('bqhk,bqkd->bqhd', p.astype(v.dtype), v_sel)

    q_pos_blocks = jnp.arange(seq).reshape(nb, Q_BLOCK)
    out = lax.map(block, (to_blocks(q), to_blocks(qi), to_blocks(wi), q_pos_blocks))
    return jnp.moveaxis(out, 0, 1).reshape(bsz, seq, DSA_WIDTH)


def memory_cross_attention(q, mem_n, w_mem_kv):
    bsz, seq, _ = q.shape
    k_m, v_m = jnp.split(mem_n @ w_mem_kv, 2, axis=-1)
    q = q.reshape(bsz, seq, XA_HEADS, XA_HEAD_DIM)
    k_m = k_m.reshape(bsz, -1, XA_HEADS, XA_HEAD_DIM)
    v_m = v_m.reshape(bsz, -1, XA_HEADS, XA_HEAD_DIM)
    logits = jnp.einsum('bqhd,bmhd->bhqm', q, k_m).astype(jnp.float32) * XA_HEAD_DIM ** -0.5
    p = jax.nn.softmax(logits, axis=-1)
    out = jnp.einsum('bhqm,bmhd->bqhd', p.astype(v_m.dtype), v_m)
    return out.reshape(bsz, seq, XA_WIDTH)


def setup_inputs(seed: int = 0) -> dict:
    key = jax.random.key(seed)
    ks = jax.random.split(key, 32)
    f32 = jnp.float32
    L = DEPTH

    def dense(k, shape, fan_in):
        return jax.random.normal(k, shape, f32) * fan_in ** -0.5

    def gain(k, shape):
        return 1.0 + 0.02 * jax.random.normal(k, shape, f32)

    x = jax.random.normal(ks[0], (BATCH, SEQ, D_MODEL), f32)
    mem = jax.random.normal(ks[1], (BATCH, MEM_LEN, D_MODEL), f32)
    positions = (jax.random.randint(ks[2], (BATCH, 1), 0, 4096, dtype=jnp.int32)
                 + jnp.arange(SEQ, dtype=jnp.int32)[None, :])
    dt0 = jnp.exp(jax.random.uniform(ks[10], (L, SSD_HEADS), f32, np.log(1e-3), np.log(1e-1)))
    dt_bias = dt0 + jnp.log(-jnp.expm1(-dt0))
    a_log = jnp.log(jax.random.uniform(ks[11], (L, SSD_HEADS), f32, 1.0, 16.0))
    return {
        'x': x,
        'mem': mem,
        'positions': positions,
        'ffn1_norm': gain(ks[3], (L, D_MODEL)),
        'w_ffn1_in': dense(ks[4], (L, D_MODEL, 2 * D_FF), D_MODEL),
        'w_ffn1_out': dense(ks[5], (L, D_FF, D_MODEL), D_FF),
        'mix_norm': gain(ks[6], (L, D_MODEL)),
        'w_in': dense(ks[7], (L, D_MODEL, IN_WIDTH), D_MODEL),
        'conv_w': dense(ks[8], (L, SSD_CONV, SSD_CONV_CH), SSD_CONV),
        'conv_b': 0.01 * jax.random.normal(ks[9], (L, SSD_CONV_CH), f32),
        'dt_bias': dt_bias,
        'a_log': a_log,
        'd_skip': 1.0 + 0.1 * jax.random.normal(ks[12], (L, SSD_HEADS), f32),
        'ssd_norm': gain(ks[13], (L, SSD_INNER)),
        'mem_norm': gain(ks[14], (L, D_MODEL)),
        'w_mem_kv': dense(ks[15], (L, D_MODEL, 2 * XA_WIDTH), D_MODEL),
        'w_br_ssd': dense(ks[16], (L, SSD_INNER, D_MODEL), SSD_INNER),
        'w_br_dsa': dense(ks[17], (L, DSA_WIDTH, D_MODEL), DSA_WIDTH),
        'w_br_mem': dense(ks[18], (L, XA_WIDTH, D_MODEL), XA_WIDTH),
        'w_out': dense(ks[19], (L, D_MODEL, D_MODEL), D_MODEL),
        'ffn2_norm': gain(ks[20], (L, D_MODEL)),
        'w_ffn2_in': dense(ks[21], (L, D_MODEL, 2 * D_FF), D_MODEL),
        'w_ffn2_out': dense(ks[22], (L, D_FF, D_MODEL), D_FF),
        'final_norm': gain(ks[23], (D_MODEL,)),
    }


def reference(x, mem, positions, ffn1_norm, w_ffn1_in, w_ffn1_out, mix_norm, w_in,
              conv_w, conv_b, dt_bias, a_log, d_skip, ssd_norm, mem_norm, w_mem_kv,
              w_br_ssd, w_br_dsa, w_br_mem, w_out, ffn2_norm, w_ffn2_in, w_ffn2_out,
              final_norm):
    bsz, seq, _ = x.shape
    cos_a, sin_a = rope_tables(positions, DSA_ROPE)
    cos_i, sin_i = rope_tables(positions, IDX_ROPE)
    for l in range(DEPTH):
        x = x + 0.5 * swiglu(rms_norm(x, ffn1_norm[l]), w_ffn1_in[l], w_ffn1_out[l])
        h = rms_norm(x, mix_norm[l])
        (z, xbc, dt_raw, q, k, v, qi, ki, wi, q_mem, gate_pre) = jnp.split(
            h @ w_in[l], IN_OFFSETS, axis=-1)
        y_ssd = ssd_mixer(z, xbc, dt_raw, conv_w[l], conv_b[l], dt_bias[l], a_log[l],
                          d_skip[l], ssd_norm[l])
        y_dsa = dsa_mixer(q, k, v, qi, ki, wi, cos_a, sin_a, cos_i, sin_i)
        y_mem = memory_cross_attention(q_mem, rms_norm(mem, mem_norm[l]), w_mem_kv[l])
        g = jax.nn.sigmoid(gate_pre.astype(jnp.float32)).astype(x.dtype)
        g = g.reshape(bsz, seq, N_BRANCH, D_MODEL)
        merged = (g[:, :, 0] * (y_ssd @ w_br_ssd[l])
                  + g[:, :, 1] * (y_dsa @ w_br_dsa[l])
                  + g[:, :, 2] * (y_mem @ w_br_mem[l]))
        x = x + merged @ w_out[l]
        x = x + 0.5 * swiglu(rms_norm(x, ffn2_norm[l]), w_ffn2_in[l], w_ffn2_out[l])
    return rms_norm(x, final_norm)
```

```python
import functools
import math

import numpy as np
import jax
import jax.numpy as jnp
from jax import lax
from jax.experimental import pallas as pl
from jax.experimental.pallas import tpu as pltpu

F32 = jnp.float32
BF16 = jnp.bfloat16

D_MODEL = 2048
MEM_LEN = 256
NORM_EPS = 1e-6
ROPE_THETA = 500000.0
D_FF = 5632

SSD_INNER = 4096
SSD_HEAD_DIM = 64
SSD_HEADS = 64
SSD_GROUPS = 8
SSD_HEADS_PER_GROUP = 8
SSD_STATE = 128
SSD_CONV = 4
SSD_CHUNK = 128
SSD_GROUP_WIDTH = SSD_HEADS_PER_GROUP * SSD_HEAD_DIM
SSD_BC = 2 * SSD_GROUPS * SSD_STATE

DSA_HEADS = 16
DSA_HEAD_DIM = 128
DSA_WIDTH = 2048
DSA_ROPE = 32
IDX_HEADS = 16
IDX_DIM = 64
IDX_ROPE = 16
DSA_TOPK_MAX = 256
DSA_QBLOCK = 128

XA_HEADS = 4
XA_HEAD_DIM = 512
XA_WIDTH = 2048

_IN_SPLITS = (SSD_INNER, SSD_INNER + SSD_BC, SSD_HEADS, DSA_WIDTH, DSA_HEAD_DIM, DSA_HEAD_DIM,
              IDX_HEADS * IDX_DIM, IDX_DIM, IDX_HEADS, XA_WIDTH, 3 * D_MODEL)
_IN_OFF = np.concatenate([[0], np.cumsum(_IN_SPLITS)]).tolist()

P_Z = 0
P_XS = 4096
P_Q = 8192
P_QMEM = 10240
P_GATE = 12288
P_BC = 18432
P_QI = 20480
P_SMALL = 21504
P_WIDTH = 22016
S_K, S_V, S_KI, S_DT, S_WI = 0, 128, 256, 320, 384

LANES = 128
V7X_VMEM_LIMIT = 56 * 1024 * 1024
INT_MIN = -(2 ** 31)


def _cparams(*sem):
    return pltpu.CompilerParams(dimension_semantics=sem, vmem_limit_bytes=V7X_VMEM_LIMIT)


def _dot(a, b):
    return jnp.dot(a, b, preferred_element_type=F32)


def _dot_nt(a, b):
    return lax.dot_general(a, b, (((1,), (1,)), ((), ())), preferred_element_type=F32)


def _sigmoid(x):
    return 1.0 / (1.0 + jnp.exp(-x))


def _rms_rows_to(x_ref, g_ref, h_ref, rows):
    n = x_ref.shape[0] // rows

    def body(r, carry):
        sl = pl.ds(pl.multiple_of(r * rows, rows), rows)
        x = x_ref[sl, :]
        ms = jnp.mean(x * x, axis=-1, keepdims=True)
        h_ref[sl, :] = (x * lax.rsqrt(ms + NORM_EPS) * g_ref[...]).astype(h_ref.dtype)
        return carry

    lax.fori_loop(0, n, body, 0)


def _norm_matmul_kernel(x_ref, g_ref, w_ref, o_ref, h_ref):
    @pl.when(pl.program_id(1) == 0)
    def _():
        _rms_rows_to(x_ref, g_ref, h_ref, min(128, x_ref.shape[0]))

    o_ref[...] = _dot(h_ref[...], w_ref[...]).astype(o_ref.dtype)


def _norm_matmul(x, g, w, *, tm, tn, out_dtype, name):
    m, d = x.shape
    n = w.shape[1]
    tm = min(tm, m)
    return pl.pallas_call(
        _norm_matmul_kernel,
        grid=(m // tm, n // tn),
        in_specs=[pl.BlockSpec((tm, d), lambda i, j: (i, 0)),
                  pl.BlockSpec((1, d), lambda i, j: (0, 0)),
                  pl.BlockSpec((d, tn), lambda i, j: (0, j))],
        out_specs=pl.BlockSpec((tm, tn), lambda i, j: (i, j)),
        out_shape=jax.ShapeDtypeStruct((m, n), out_dtype),
        scratch_shapes=[pltpu.VMEM((tm, d), BF16)],
        compiler_params=_cparams("parallel", "arbitrary"),
        name=name,
    )(x, g.reshape(1, d), w)


def _ffn_kernel(x_ref, g_ref, wg_ref, wu_ref, wo_ref, fg_ref, o_ref, h_ref, *, final_norm):
    j = pl.program_id(1)

    @pl.when(j == 0)
    def _():
        _rms_rows_to(x_ref, g_ref, h_ref, min(128, x_ref.shape[0]))

    h = h_ref[...]
    gate = _dot(h, wg_ref[...])
    up = _dot(h, wu_ref[...])
    act = (gate * _sigmoid(gate) * up).astype(BF16)
    upd = 0.5 * _dot(act, wo_ref[...])

    @pl.when(j == 0)
    def _():
        o_ref[...] = x_ref[...] + upd

    @pl.when(j > 0)
    def _():
        o_ref[...] += upd

    if final_norm:
        @pl.when(j == pl.num_programs(1) - 1)
        def _():
            _rms_rows_to(o_ref, fg_ref, o_ref, min(128, o_ref.shape[0]))


def _ffn(x, g, w_in, w_out, final_g, *, final_norm, tm=512, tf=512):
    m, d = x.shape
    f = w_out.shape[0]
    tm = min(tm, m)
    nf = f // tf
    return pl.pallas_call(
        functools.partial(_ffn_kernel, final_norm=final_norm),
        grid=(m // tm, nf),
        in_specs=[pl.BlockSpec((tm, d), lambda i, j: (i, 0)),
                  pl.BlockSpec((1, d), lambda i, j: (0, 0)),
                  pl.BlockSpec((d, tf), lambda i, j: (0, j)),
                  pl.BlockSpec((d, tf), lambda i, j: (0, j + nf)),
                  pl.BlockSpec((tf, d), lambda i, j: (j, 0)),
                  pl.BlockSpec((1, d), lambda i, j: (0, 0))],
        out_specs=pl.BlockSpec((tm, d), lambda i, j: (i, 0)),
        out_shape=jax.ShapeDtypeStruct((m, d), F32),
        scratch_shapes=[pltpu.VMEM((tm, d), BF16)],
        compiler_params=_cparams("parallel", "arbitrary"),
        name="ffn_final" if final_norm else "ffn",
    )(x, g.reshape(1, d), w_in, w_in, w_out, final_g.reshape(1, d))


def _merge_kernel(x_ref, ys_ref, yd_ref, ym_ref, g0_ref, g1_ref, g2_ref,
                  ws_ref, wd_ref, wm_ref, wo_ref, o_ref):
    j = pl.program_id(1)
    merged = (_sigmoid(g0_ref[...]) * _dot(ys_ref[...], ws_ref[...])
              + _sigmoid(g1_ref[...]) * _dot(yd_ref[...], wd_ref[...])
              + _sigmoid(g2_ref[...]) * _dot(ym_ref[...], wm_ref[...]))
    upd = _dot(merged.astype(BF16), wo_ref[...])

    @pl.when(j == 0)
    def _():
        o_ref[...] = x_ref[...] + upd

    @pl.when(j > 0)
    def _():
        o_ref[...] += upd


def _merge(x, y_ssd, y_dsa, y_mem, proj, w_s, w_d, w_m, w_o, *, tm=512, tn=256):
    m, d = x.shape
    tm = min(tm, m)
    gate_blk = P_GATE // tn
    per_branch = d // tn
    return pl.pallas_call(
        _merge_kernel,
        grid=(m // tm, d // tn),
        in_specs=[pl.BlockSpec((tm, d), lambda i, j: (i, 0)),
                  pl.BlockSpec((tm, SSD_INNER), lambda i, j: (i, 0)),
                  pl.BlockSpec((tm, DSA_WIDTH), lambda i, j: (i, 0)),
                  pl.BlockSpec((tm, XA_WIDTH), lambda i, j: (i, 0)),
                  pl.BlockSpec((tm, tn), lambda i, j: (i, gate_blk + j)),
                  pl.BlockSpec((tm, tn), lambda i, j: (i, gate_blk + per_branch + j)),
                  pl.BlockSpec((tm, tn), lambda i, j: (i, gate_blk + 2 * per_branch + j)),
                  pl.BlockSpec((SSD_INNER, tn), lambda i, j: (0, j)),
                  pl.BlockSpec((DSA_WIDTH, tn), lambda i, j: (0, j)),
                  pl.BlockSpec((XA_WIDTH, tn), lambda i, j: (0, j)),
                  pl.BlockSpec((tn, d), lambda i, j: (j, 0))],
        out_specs=pl.BlockSpec((tm, d), lambda i, j: (i, 0)),
        out_shape=jax.ShapeDtypeStruct((m, d), F32),
        compiler_params=_cparams("parallel", "arbitrary"),
        name="merge",
    )(x, y_ssd, y_dsa, y_mem, proj, proj, proj, w_s, w_d, w_m, w_o)


def _memattn_kernel(q_ref, k_ref, v_ref, o_ref, *, rows):
    k = k_ref[...]
    v = v_ref[...]
    scale = XA_HEAD_DIM ** -0.5

    def body(r, carry):
        sl = pl.ds(pl.multiple_of(r * rows, rows), rows)
        q = q_ref[sl, :].astype(BF16)
        logits = _dot_nt(q, k) * scale
        mx = jnp.max(logits, axis=-1, keepdims=True)
        e = jnp.exp(logits - mx)
        p = e * (1.0 / jnp.sum(e, axis=-1, keepdims=True))
        o_ref[sl, :] = _dot(p.astype(BF16), v).astype(o_ref.dtype)
        return carry

    lax.fori_loop(0, q_ref.shape[0] // rows, body, 0)


def _memattn(proj, kv, *, bsz, seq):
    qblk = P_QMEM // XA_HEAD_DIM
    return pl.pallas_call(
        functools.partial(_memattn_kernel, rows=min(256, seq)),
        grid=(bsz, XA_HEADS),
        in_specs=[pl.BlockSpec((seq, XA_HEAD_DIM), lambda b, h: (b, qblk + h)),
                  pl.BlockSpec((MEM_LEN, XA_HEAD_DIM), lambda b, h: (b, h)),
                  pl.BlockSpec((MEM_LEN, XA_HEAD_DIM), lambda b, h: (b, XA_HEADS + h))],
        out_specs=pl.BlockSpec((seq, XA_HEAD_DIM), lambda b, h: (b, h)),
        out_shape=jax.ShapeDtypeStruct((bsz * seq, XA_WIDTH), BF16),
        compiler_params=_cparams("parallel", "parallel"),
        name="memattn",
    )(proj, kv, kv)


def _rope_tables_kernel(pos_ref, ca_ref, sa_ref, ci_ref, si_ref, *, rows):
    lane = lax.broadcasted_iota(jnp.int32, (1, LANES), 1)
    ln_theta = math.log(ROPE_THETA)
    inv_a = jnp.exp((lane & (DSA_ROPE // 2 - 1)).astype(F32) * (-2.0 * ln_theta / DSA_ROPE))
    inv_i = jnp.exp((lane & (IDX_ROPE // 2 - 1)).astype(F32) * (-2.0 * ln_theta / IDX_ROPE))
    lane_i = lane & (IDX_DIM - 1)

    def body(r, carry):
        sl = pl.ds(pl.multiple_of(r * rows, rows), rows)
        pos = pos_ref[sl, :].astype(F32)
        ang = pos * inv_a
        c, s = jnp.cos(ang), jnp.sin(ang)
        ca_ref[sl, :] = jnp.where(lane < DSA_ROPE, c, 1.0)
        sa_ref[sl, :] = jnp.where(lane < DSA_ROPE // 2, -s, jnp.where(lane < DSA_ROPE, s, 0.0))
        ang = pos * inv_i
        c, s = jnp.cos(ang), jnp.sin(ang)
        ci_ref[sl, :] = jnp.where(lane_i < IDX_ROPE, c, 1.0)
        si_ref[sl, :] = jnp.where(lane_i < IDX_ROPE // 2, -s, jnp.where(lane_i < IDX_ROPE, s, 0.0))
        return carry

    lax.fori_loop(0, pos_ref.shape[0] // rows, body, 0)


def _rope_tables(positions):
    bsz, seq = positions.shape
    tab = jax.ShapeDtypeStruct((bsz * seq, LANES), F32)
    spec = pl.BlockSpec((seq, LANES), lambda b: (b, 0))
    return pl.pallas_call(
        functools.partial(_rope_tables_kernel, rows=64),
        grid=(bsz,),
        in_specs=[pl.BlockSpec((seq, 1), lambda b: (b, 0))],
        out_specs=[spec, spec, spec, spec],
        out_shape=[tab, tab, tab, tab],
        compiler_params=_cparams("parallel"),
        name="rope_tables",
    )(positions.reshape(bsz * seq, 1))


def _swap_attn(x, lane):
    h = DSA_ROPE // 2
    return jnp.where(lane < h, pltpu.roll(x, LANES - h, 1), pltpu.roll(x, h, 1))


def _swap_idx(x, lane_i):
    h = IDX_ROPE // 2
    return jnp.where(lane_i < h, pltpu.roll(x, LANES - h, 1), pltpu.roll(x, h, 1))


def _dsa_kernel(q_ref, qi_ref, sm_ref, ca_ref, sa_ref, ci_ref, si_ref, o_ref,
                k_ref, kil_ref, kih_ref, v_ref, score_ref, key_ref, bias_ref, qs_ref, os_ref,
                *, topk, tq, heads_per_step):
    qb = pl.program_id(1)
    seq = sm_ref.shape[0]
    lane = lax.broadcasted_iota(jnp.int32, (1, LANES), 1)
    lane_i = lane & (IDX_DIM - 1)

    @pl.when(qb == 0)
    def _prep():
        rows = min(256, seq)

        def body(r, carry):
            sl = pl.ds(pl.multiple_of(r * rows, rows), rows)
            k = sm_ref[sl, S_K:S_K + LANES]
            k_ref[sl, :] = (k * ca_ref[sl, :] + _swap_attn(k, lane) * sa_ref[sl, :]).astype(BF16)
            v_ref[sl, :] = sm_ref[sl, S_V:S_V + LANES].astype(BF16)
            ki = sm_ref[sl, S_KI:S_KI + LANES]
            kir = ki * ci_ref[sl, :] + _swap_idx(ki, lane_i) * si_ref[sl, :]
            kil = jnp.where(lane < IDX_DIM, kir, 0.0)
            kil_ref[sl, :] = kil.astype(BF16)
            kih_ref[sl, :] = pltpu.roll(kil, IDX_DIM, 1).astype(BF16)
            return carry

        lax.fori_loop(0, seq // rows, body, 0)

    row0 = pl.multiple_of(qb * tq, tq)
    qrows = pl.ds(row0, tq)
    ca_q, sa_q = ca_ref[qrows, :], sa_ref[qrows, :]
    ci_q, si_q = ci_ref[qrows, :], si_ref[qrows, :]

    w_t = sm_ref[qrows, S_WI:S_WI + LANES].T * (IDX_DIM ** -0.5 * IDX_HEADS ** -0.5)
    score_ref[...] = jnp.zeros_like(score_ref)
    for pr in range(IDX_HEADS // 2):
        x = qi_ref[:, pr * LANES:(pr + 1) * LANES]
        xr = (x * ci_q + _swap_idx(x, lane_i) * si_q).astype(BF16)
        s_lo = _dot_nt(kil_ref[...], xr)
        s_hi = _dot_nt(kih_ref[...], xr)
        score_ref[...] += (jnp.maximum(s_lo, 0.0) * w_t[2 * pr:2 * pr + 1, :]
                           + jnp.maximum(s_hi, 0.0) * w_t[2 * pr + 1:2 * pr + 2, :])

    kpos = lax.broadcasted_iota(jnp.int32, (seq, tq), 0)
    qpos = row0 + lax.broadcasted_iota(jnp.int32, (seq, tq), 1)
    causal = kpos <= qpos
    bits = lax.bitcast_convert_type(score_ref[...], jnp.int32)
    skey = jnp.where(bits >= 0, bits, bits ^ jnp.int32(0x7FFFFFFF))
    key_ref[...] = jnp.where(causal, skey, jnp.int32(INT_MIN))

    def search(i, thr):
        cand = thr ^ lax.shift_left(jnp.int32(1), 31 - i)
        cnt = jnp.sum(jnp.where(key_ref[...] >= cand, 1.0, 0.0), axis=0, keepdims=True)
        return jnp.where(cnt >= float(topk), cand, thr)

    thr = lax.fori_loop(0, 32, search, jnp.full((1, tq), INT_MIN, jnp.int32))
    bias = jnp.where(key_ref[...] >= thr, jnp.where(causal, 0.0, -jnp.inf), -jnp.inf)
    for kb in range(seq // LANES):
        bias_ref[:, kb * LANES:(kb + 1) * LANES] = bias[kb * LANES:(kb + 1) * LANES, :].T

    for h in range(DSA_HEADS):
        x = q_ref[:, h * LANES:(h + 1) * LANES]
        qs_ref[h * tq:(h + 1) * tq, :] = (x * ca_q + _swap_attn(x, lane) * sa_q).astype(BF16)

    rows = heads_per_step * tq
    scale = DSA_HEAD_DIM ** -0.5

    def attend(g, carry):
        sl = pl.ds(pl.multiple_of(g * rows, rows), rows)
        logits = _dot_nt(qs_ref[sl, :], k_ref[...]) * scale
        logits = logits + jnp.concatenate([bias_ref[...]] * heads_per_step, axis=0)
        mx = jnp.max(logits, axis=-1, keepdims=True)
        e = jnp.exp(logits - mx)
        p = e * (1.0 / jnp.sum(e, axis=-1, keepdims=True))
        os_ref[sl, :] = _dot(p.astype(BF16), v_ref[...])
        return carry

    lax.fori_loop(0, DSA_HEADS // heads_per_step, attend, 0)
    for h in range(DSA_HEADS):
        o_ref[:, h * LANES:(h + 1) * LANES] = os_ref[h * tq:(h + 1) * tq, :].astype(o_ref.dtype)


def _dsa(proj, tables, *, bsz, seq):
    tq = DSA_QBLOCK
    topk = min(DSA_TOPK_MAX, seq // 4)
    nq = seq // tq
    tab_spec = pl.BlockSpec((seq, LANES), lambda b, i: (b, 0))
    return pl.pallas_call(
        functools.partial(_dsa_kernel, topk=topk, tq=tq, heads_per_step=4),
        grid=(bsz, nq),
        in_specs=[pl.BlockSpec((tq, DSA_WIDTH), lambda b, i: (b * nq + i, P_Q // DSA_WIDTH)),
                  pl.BlockSpec((tq, IDX_HEADS * IDX_DIM), lambda b, i: (b * nq + i, P_QI // (IDX_HEADS * IDX_DIM))),
                  pl.BlockSpec((seq, 512), lambda b, i: (b, P_SMALL // 512)),
                  tab_spec, tab_spec, tab_spec, tab_spec],
        out_specs=pl.BlockSpec((tq, DSA_WIDTH), lambda b, i: (b * nq + i, 0)),
        out_shape=jax.ShapeDtypeStruct((bsz * seq, DSA_WIDTH), BF16),
        scratch_shapes=[pltpu.VMEM((seq, LANES), BF16),
                        pltpu.VMEM((seq, LANES), BF16),
                        pltpu.VMEM((seq, LANES), BF16),
                        pltpu.VMEM((seq, LANES), BF16),
                        pltpu.VMEM((seq, tq), F32),
                        pltpu.VMEM((seq, tq), jnp.int32),
                        pltpu.VMEM((tq, seq), F32),
                        pltpu.VMEM((DSA_HEADS * tq, LANES), BF16),
                        pltpu.VMEM((DSA_HEADS * tq, LANES), F32)],
        compiler_params=_cparams("parallel", "arbitrary"),
        name="dsa",
    )(proj, proj, proj, *tables)


def _split_bf16(x, parts):
    out = []
    for _ in range(parts - 1):
        hi = x.astype(BF16)
        out.append(hi)
        x = x - hi.astype(F32)
    out.append(x.astype(BF16))
    return out


def _ssd_kernel(z_ref, xs_ref, bc_ref, sm_ref, cwx_ref, cwb_ref, cbx_ref, cbb_ref,
                dtb_ref, alog_ref, dskip_ref, ng_ref, e_ref, o_ref,
                xpad_ref, bpad_ref, state_ref, y_ref):
    c = pl.program_id(1)
    q = SSD_CHUNK
    pad = 8

    @pl.when(c == 0)
    def _():
        xpad_ref[0:pad, :] = jnp.zeros((pad, SSD_INNER), F32)
        bpad_ref[0:pad, :] = jnp.zeros((pad, SSD_BC), F32)
        state_ref[...] = jnp.zeros_like(state_ref)

    xpad_ref[pad:pad + q, :] = xs_ref[...]
    bpad_ref[pad:pad + q, :] = bc_ref[...]

    def conv_silu(p_ref, w_ref, b_ref):
        acc = b_ref[...] + w_ref[SSD_CONV - 1:SSD_CONV, :] * p_ref[pad:pad + q, :]
        for j in range(1, SSD_CONV):
            acc = acc + w_ref[SSD_CONV - 1 - j:SSD_CONV - j, :] * p_ref[pad - j:pad - j + q, :]
        return acc * _sigmoid(acc)

    xs = conv_silu(xpad_ref, cwx_ref, cbx_ref)
    bc = conv_silu(bpad_ref, cwb_ref, cbb_ref)
    xpad_ref[0:pad, :] = xs_ref[q - pad:q, :]
    bpad_ref[0:pad, :] = bc_ref[q - pad:q, :]

    pre = sm_ref[:, S_DT:S_DT + SSD_HEADS] + dtb_ref[...]
    dt = jnp.maximum(pre, 0.0) + jnp.log1p(jnp.exp(-jnp.abs(pre)))
    da = dt * (-jnp.exp(alog_ref[...]))
    li = lax.broadcasted_iota(jnp.int32, (q, q), 0)
    si = lax.broadcasted_iota(jnp.int32, (q, q), 1)
    causal = li >= si
    tri = jnp.where(causal, 1.0, 0.0).astype(BF16)
    a_cs = sum(_dot(tri, part) for part in _split_bf16(da, 3))
    a_cs_t = jnp.concatenate([a_cs, jnp.zeros_like(a_cs)], axis=1).T

    expand = e_ref[...]
    dt_x = sum(_dot(part, expand) for part in _split_bf16(dt, 2))
    acs_x = sum(_dot(part, expand) for part in _split_bf16(a_cs, 3))
    ea_x = jnp.exp(acs_x)
    dte_x = jnp.exp(acs_x[q - 1:q, :] - acs_x)
    xdt = xs * dt_x
    xdt_b = xdt.astype(BF16)
    xw_b = (xdt * dte_x).astype(BF16)
    cd_x = ea_x[q - 1:q, :]

    gw = SSD_GROUP_WIDTH
    for g in range(SSD_GROUPS):
        bm = bc[:, g * SSD_STATE:(g + 1) * SSD_STATE]
        cm_b = bc[:, (SSD_GROUPS + g) * SSD_STATE:(SSD_GROUPS + g + 1) * SSD_STATE].astype(BF16)
        cb = _dot_nt(cm_b, bm.astype(BF16))
        st = state_ref[g]
        y_ref[:, g * gw:(g + 1) * gw] = _dot(cm_b, st.astype(BF16)) * ea_x[:, g * gw:(g + 1) * gw]
        for r in range(SSD_HEADS_PER_GROUP):
            h = g * SSD_HEADS_PER_GROUP + r
            seg = a_cs[:, h:h + 1] - a_cs_t[h:h + 1, :]
            decay = jnp.exp(jnp.where(causal, seg, -jnp.inf))
            lo, hi = h * SSD_HEAD_DIM, (h + 1) * SSD_HEAD_DIM
            y_ref[:, lo:hi] += _dot((cb * decay).astype(BF16), xdt_b[:, lo:hi])
        upd = _dot(bm.T.astype(BF16), xw_b[:, g * gw:(g + 1) * gw])
        state_ref[g] = st * cd_x[:, g * gw:(g + 1) * gw] + upd

    z = z_ref[...]
    y = (y_ref[...] + xs * dskip_ref[...]) * (z * _sigmoid(z))
    ms = jnp.mean(y * y, axis=-1, keepdims=True)
    o_ref[...] = (y * lax.rsqrt(ms + NORM_EPS) * ng_ref[...]).astype(o_ref.dtype)


def _ssd(proj, conv_w, conv_b, dt_bias, a_log, d_skip, norm_g, *, bsz, seq):
    q = SSD_CHUNK
    nc = seq // q
    expand = jnp.asarray(np.repeat(np.eye(SSD_HEADS, dtype=np.float32), SSD_HEAD_DIM, axis=1), BF16)
    row = lambda b, c: (b * nc + c)
    const = lambda b, c: (0, 0)
    return pl.pallas_call(
        _ssd_kernel,
        grid=(bsz, nc),
        in_specs=[pl.BlockSpec((q, SSD_INNER), lambda b, c: (row(b, c), P_Z // SSD_INNER)),
                  pl.BlockSpec((q, SSD_INNER), lambda b, c: (row(b, c), P_XS // SSD_INNER)),
                  pl.BlockSpec((q, SSD_BC), lambda b, c: (row(b, c), P_BC // SSD_BC)),
                  pl.BlockSpec((q, 512), lambda b, c: (row(b, c), P_SMALL // 512)),
                  pl.BlockSpec((SSD_CONV, SSD_INNER), const),
                  pl.BlockSpec((SSD_CONV, SSD_BC), const),
                  pl.BlockSpec((1, SSD_INNER), const),
                  pl.BlockSpec((1, SSD_BC), const),
                  pl.BlockSpec((1, SSD_HEADS), const),
                  pl.BlockSpec((1, SSD_HEADS), const),
                  pl.BlockSpec((1, SSD_INNER), const),
                  pl.BlockSpec((1, SSD_INNER), const),
                  pl.BlockSpec((SSD_HEADS, SSD_INNER), const)],
        out_specs=pl.BlockSpec((q, SSD_INNER), lambda b, c: (row(b, c), 0)),
        out_shape=jax.ShapeDtypeStruct((bsz * seq, SSD_INNER), BF16),
        scratch_shapes=[pltpu.VMEM((q + 8, SSD_INNER), F32),
                        pltpu.VMEM((q + 8, SSD_BC), F32),
                        pltpu.VMEM((SSD_GROUPS, SSD_STATE, SSD_GROUP_WIDTH), F32),
                        pltpu.VMEM((q, SSD_INNER), F32)],
        compiler_params=_cparams("parallel", "arbitrary"),
        name="ssd",
    )(proj, proj, proj, proj,
      conv_w[:, :SSD_INNER], conv_w[:, SSD_INNER:], conv_b[None, :SSD_INNER], conv_b[None, SSD_INNER:],
      dt_bias[None, :], a_log[None, :], jnp.repeat(d_skip, SSD_HEAD_DIM)[None, :], norm_g[None, :], expand)


def _regroup_w_in(w_in):
    o = _IN_OFF
    sl = lambda a, b: w_in[:, :, a:b]
    z, xbc, dt, q, k, v, qi, ki, wi, qm, gate = (sl(o[i], o[i + 1]) for i in range(11))
    small_pad = jnp.zeros(w_in.shape[:2] + (512 - (S_WI + IDX_HEADS),), w_in.dtype)
    cols = [z, xbc[:, :, :SSD_INNER], q, qm, gate, xbc[:, :, SSD_INNER:], qi, k, v, ki, dt, wi, small_pad]
    out = jnp.concatenate(cols, axis=-1).astype(BF16)
    assert out.shape[-1] == P_WIDTH
    return out


def kernel(x, mem, positions, ffn1_norm, w_ffn1_in, w_ffn1_out, mix_norm, w_in, conv_w, conv_b, dt_bias, a_log,
           d_skip, ssd_norm, mem_norm, w_mem_kv, w_br_ssd, w_br_dsa, w_br_mem, w_out, ffn2_norm, w_ffn2_in,
           w_ffn2_out, final_norm):
    bsz, seq, d = x.shape
    depth = w_in.shape[0]
    assert d == D_MODEL and seq % DSA_QBLOCK == 0 and seq % SSD_CHUNK == 0 and mem.shape[1] == MEM_LEN

    w_p = _regroup_w_in(w_in)
    w1i, w1o = w_ffn1_in.astype(BF16), w_ffn1_out.astype(BF16)
    w2i, w2o = w_ffn2_in.astype(BF16), w_ffn2_out.astype(BF16)
    wkv = w_mem_kv.astype(BF16)
    wbs, wbd, wbm, wo = (w.astype(BF16) for w in (w_br_ssd, w_br_dsa, w_br_mem, w_out))

    tables = _rope_tables(positions)
    xf = x.reshape(bsz * seq, d)
    memf = mem.reshape(bsz * MEM_LEN, d)
    for l in range(depth):
        xf = _ffn(xf, ffn1_norm[l], w1i[l], w1o[l], final_norm, final_norm=False)
        proj = _norm_matmul(xf, mix_norm[l], w_p[l], tm=1024, tn=512, out_dtype=F32, name="in_proj")
        y_ssd = _ssd(proj, conv_w[l], conv_b[l], dt_bias[l], a_log[l], d_skip[l], ssd_norm[l], bsz=bsz, seq=seq)
        y_dsa = _dsa(proj, tables, bsz=bsz, seq=seq)
        kv = _norm_matmul(memf, mem_norm[l], wkv[l], tm=1024, tn=512, out_dtype=BF16, name="mem_kv")
        y_mem = _memattn(proj, kv, bsz=bsz, seq=seq)
        xf = _merge(xf, y_ssd, y_dsa, y_mem, proj, wbs[l], wbd[l], wbm[l], wo[l])
        xf = _ffn(xf, ffn2_norm[l], w2i[l], w2o[l], final_norm, final_norm=(l == depth - 1))
    return xf.reshape(bsz, seq, d)
```

```python
import functools
import math

import numpy as np
import jax
import jax.numpy as jnp
from jax import lax
from jax.experimental import pallas as pl
from jax.experimental.pallas import tpu as pltpu

F32 = jnp.float32
BF16 = jnp.bfloat16

D_MODEL = 2048
MEM_LEN = 256
NORM_EPS = 1e-6
ROPE_THETA = 500000.0
D_FF = 5632

SSD_INNER = 4096
SSD_HEAD_DIM = 64
SSD_HEADS = 64
SSD_GROUPS = 8
SSD_HEADS_PER_GROUP = 8
SSD_STATE = 128
SSD_CONV = 4
SSD_CHUNK = 128
SSD_GROUP_WIDTH = SSD_HEADS_PER_GROUP * SSD_HEAD_DIM
SSD_BC = 2 * SSD_GROUPS * SSD_STATE

DSA_HEADS = 16
DSA_HEAD_DIM = 128
DSA_WIDTH = 2048
DSA_ROPE = 32
IDX_HEADS = 16
IDX_DIM = 64
IDX_ROPE = 16
DSA_TOPK_MAX = 256
DSA_QBLOCK = 128

XA_HEADS = 4
XA_HEAD_DIM = 512
XA_WIDTH = 2048

_IN_SPLITS = (SSD_INNER, SSD_INNER + SSD_BC, SSD_HEADS, DSA_WIDTH, DSA_HEAD_DIM, DSA_HEAD_DIM,
              IDX_HEADS * IDX_DIM, IDX_DIM, IDX_HEADS, XA_WIDTH, 3 * D_MODEL)
_IN_OFF = np.concatenate([[0], np.cumsum(_IN_SPLITS)]).tolist()

P_Z = 0
P_XS = 4096
P_Q = 8192
P_QMEM = 10240
P_GATE = 12288
P_BC = 18432
P_QI = 20480
P_SMALL = 21504
P_WIDTH = 22016
S_K, S_V, S_KI, S_DT, S_WI = 0, 128, 256, 320, 384

LANES = 128
V7X_VMEM_LIMIT = 56 * 1024 * 1024
INT_MIN = -(2 ** 31)


def _cparams(*sem):
    return pltpu.CompilerParams(dimension_semantics=sem, vmem_limit_bytes=V7X_VMEM_LIMIT)


def _dot(a, b):
    return jnp.dot(a, b, preferred_element_type=F32)


def _dot_nt(a, b):
    return lax.dot_general(a, b, (((1,), (1,)), ((), ())), preferred_element_type=F32)


def _sigmoid(x):
    return 1.0 / (1.0 + jnp.exp(-x))


def _rms_rows_to(x_ref, g_ref, h_ref, rows):
    n = x_ref.shape[0] // rows

    def body(r, carry):
        sl = pl.ds(pl.multiple_of(r * rows, rows), rows)
        x = x_ref[sl, :]
        ms = jnp.mean(x * x, axis=-1, keepdims=True)
        h_ref[sl, :] = (x * lax.rsqrt(ms + NORM_EPS) * g_ref[...]).astype(h_ref.dtype)
        return carry

    lax.fori_loop(0, n, body, 0)


def _norm_matmul_kernel(x_ref, g_ref, w_ref, o_ref, h_ref):
    @pl.when(pl.program_id(1) == 0)
    def _():
        _rms_rows_to(x_ref, g_ref, h_ref, min(128, x_ref.shape[0]))

    o_ref[...] = _dot(h_ref[...], w_ref[...]).astype(o_ref.dtype)


def _norm_matmul(x, g, w, layer, *, tm, tn, out_dtype, name):
    m, d = x.shape
    n = w.shape[2]
    tm = min(tm, m)
    return pl.pallas_call(
        _norm_matmul_kernel,
        grid=(m // tm, n // tn),
        in_specs=[pl.BlockSpec((tm, d), lambda i, j: (i, 0)),
                  pl.BlockSpec((1, d), lambda i, j: (0, 0)),
                  pl.BlockSpec((None, d, tn), lambda i, j: (layer, 0, j))],
        out_specs=pl.BlockSpec((tm, tn), lambda i, j: (i, j)),
        out_shape=jax.ShapeDtypeStruct((m, n), out_dtype),
        scratch_shapes=[pltpu.VMEM((tm, d), BF16)],
        compiler_params=_cparams("parallel", "arbitrary"),
        name=name,
    )(x, g.reshape(1, d), w)


def _ffn_kernel(x_ref, g_ref, wg_ref, wu_ref, wo_ref, fg_ref, o_ref, h_ref, *, final_norm):
    j = pl.program_id(1)

    @pl.when(j == 0)
    def _():
        _rms_rows_to(x_ref, g_ref, h_ref, min(128, x_ref.shape[0]))
        o_ref[...] = x_ref[...]

    h = h_ref[...]
    gate = _dot(h, wg_ref[...])
    up = _dot(h, wu_ref[...])
    act = (gate * _sigmoid(gate) * up).astype(BF16)
    o_ref[...] += 0.5 * _dot(act, wo_ref[...])

    if final_norm:
        @pl.when(j == pl.num_programs(1) - 1)
        def _():
            _rms_rows_to(o_ref, fg_ref, o_ref, min(128, o_ref.shape[0]))


def _ffn(x, g, w_in, w_out, final_g, layer, *, final_norm, tm=512, tf=512):
    m, d = x.shape
    f = w_out.shape[1]
    tm = min(tm, m)
    nf = f // tf
    return pl.pallas_call(
        functools.partial(_ffn_kernel, final_norm=final_norm),
        grid=(m // tm, nf),
        in_specs=[pl.BlockSpec((tm, d), lambda i, j: (i, 0)),
                  pl.BlockSpec((1, d), lambda i, j: (0, 0)),
                  pl.BlockSpec((None, d, tf), lambda i, j: (layer, 0, j)),
                  pl.BlockSpec((None, d, tf), lambda i, j: (layer, 0, j + nf)),
                  pl.BlockSpec((None, tf, d), lambda i, j: (layer, j, 0)),
                  pl.BlockSpec((1, d), lambda i, j: (0, 0))],
        out_specs=pl.BlockSpec((tm, d), lambda i, j: (i, 0)),
        out_shape=jax.ShapeDtypeStruct((m, d), F32),
        scratch_shapes=[pltpu.VMEM((tm, d), BF16)],
        compiler_params=_cparams("parallel", "arbitrary"),
        name="ffn_final" if final_norm else "ffn",
    )(x, g.reshape(1, d), w_in, w_in, w_out, final_g.reshape(1, d))


def _merge_kernel(x_ref, ys_ref, yd_ref, ym_ref, g0_ref, g1_ref, g2_ref,
                  ws_ref, wd_ref, wm_ref, wo_ref, o_ref):
    @pl.when(pl.program_id(1) == 0)
    def _():
        o_ref[...] = x_ref[...]

    merged = (_sigmoid(g0_ref[...]) * _dot(ys_ref[...], ws_ref[...])
              + _sigmoid(g1_ref[...]) * _dot(yd_ref[...], wd_ref[...])
              + _sigmoid(g2_ref[...]) * _dot(ym_ref[...], wm_ref[...]))
    o_ref[...] += _dot(merged.astype(BF16), wo_ref[...])


def _merge(x, y_ssd, y_dsa, y_mem, proj, w_s, w_d, w_m, w_o, layer, *, tm=512, tn=256):
    m, d = x.shape
    tm = min(tm, m)
    gate_blk = P_GATE // tn
    per_branch = d // tn
    return pl.pallas_call(
        _merge_kernel,
        grid=(m // tm, d // tn),
        in_specs=[pl.BlockSpec((tm, d), lambda i, j: (i, 0)),
                  pl.BlockSpec((tm, SSD_INNER), lambda i, j: (i, 0)),
                  pl.BlockSpec((tm, DSA_WIDTH), lambda i, j: (i, 0)),
                  pl.BlockSpec((tm, XA_WIDTH), lambda i, j: (i, 0)),
                  pl.BlockSpec((tm, tn), lambda i, j: (i, gate_blk + j)),
                  pl.BlockSpec((tm, tn), lambda i, j: (i, gate_blk + per_branch + j)),
                  pl.BlockSpec((tm, tn), lambda i, j: (i, gate_blk + 2 * per_branch + j)),
                  pl.BlockSpec((None, SSD_INNER, tn), lambda i, j: (layer, 0, j)),
                  pl.BlockSpec((None, DSA_WIDTH, tn), lambda i, j: (layer, 0, j)),
                  pl.BlockSpec((None, XA_WIDTH, tn), lambda i, j: (layer, 0, j)),
                  pl.BlockSpec((None, tn, d), lambda i, j: (layer, j, 0))],
        out_specs=pl.BlockSpec((tm, d), lambda i, j: (i, 0)),
        out_shape=jax.ShapeDtypeStruct((m, d), F32),
        compiler_params=_cparams("parallel", "arbitrary"),
        name="merge",
    )(x, y_ssd, y_dsa, y_mem, proj, proj, proj, w_s, w_d, w_m, w_o)


def _memattn_kernel(q_ref, k_ref, v_ref, o_ref, *, rows):
    k = k_ref[...]
    v = v_ref[...]
    scale = XA_HEAD_DIM ** -0.5

    def body(r, carry):
        sl = pl.ds(pl.multiple_of(r * rows, rows), rows)
        q = q_ref[sl, :].astype(BF16)
        logits = _dot_nt(q, k) * scale
        mx = jnp.max(logits, axis=-1, keepdims=True)
        e = jnp.exp(logits - mx)
        p = e * (1.0 / jnp.sum(e, axis=-1, keepdims=True))
        o_ref[sl, :] = _dot(p.astype(BF16), v).astype(o_ref.dtype)
        return carry

    lax.fori_loop(0, q_ref.shape[0] // rows, body, 0)


def _memattn(proj, kv, *, bsz, seq):
    qblk = P_QMEM // XA_HEAD_DIM
    return pl.pallas_call(
        functools.partial(_memattn_kernel, rows=min(256, seq)),
        grid=(bsz, XA_HEADS),
        in_specs=[pl.BlockSpec((seq, XA_HEAD_DIM), lambda b, h: (b, qblk + h)),
                  pl.BlockSpec((MEM_LEN, XA_HEAD_DIM), lambda b, h: (b, h)),
                  pl.BlockSpec((MEM_LEN, XA_HEAD_DIM), lambda b, h: (b, XA_HEADS + h))],
        out_specs=pl.BlockSpec((seq, XA_HEAD_DIM), lambda b, h: (b, h)),
        out_shape=jax.ShapeDtypeStruct((bsz * seq, XA_WIDTH), BF16),
        compiler_params=_cparams("parallel", "parallel"),
        name="memattn",
    )(proj, kv, kv)


def _rope_tables_kernel(pos_ref, ca_ref, sa_ref, ci_ref, si_ref, *, rows):
    lane = lax.broadcasted_iota(jnp.int32, (1, LANES), 1)
    ln_theta = math.log(ROPE_THETA)
    inv_a = jnp.exp((lane & (DSA_ROPE // 2 - 1)).astype(F32) * (-2.0 * ln_theta / DSA_ROPE))
    inv_i = jnp.exp((lane & (IDX_ROPE // 2 - 1)).astype(F32) * (-2.0 * ln_theta / IDX_ROPE))
    lane_i = lane & (IDX_DIM - 1)

    def body(r, carry):
        sl = pl.ds(pl.multiple_of(r * rows, rows), rows)
        pos = pos_ref[sl, :].astype(F32)
        ang = pos * inv_a
        c, s = jnp.cos(ang), jnp.sin(ang)
        ca_ref[sl, :] = jnp.where(lane < DSA_ROPE, c, 1.0)
        sa_ref[sl, :] = jnp.where(lane < DSA_ROPE // 2, -s, jnp.where(lane < DSA_ROPE, s, 0.0))
        ang = pos * inv_i
        c, s = jnp.cos(ang), jnp.sin(ang)
        ci_ref[sl, :] = jnp.where(lane_i < IDX_ROPE, c, 1.0)
        si_ref[sl, :] = jnp.where(lane_i < IDX_ROPE // 2, -s, jnp.where(lane_i < IDX_ROPE, s, 0.0))
        return carry

    lax.fori_loop(0, pos_ref.shape[0] // rows, body, 0)


def _rope_tables(positions):
    bsz, seq = positions.shape
    tab = jax.ShapeDtypeStruct((bsz * seq, LANES), F32)
    spec = pl.BlockSpec((seq, LANES), lambda b: (b, 0))
    return pl.pallas_call(
        functools.partial(_rope_tables_kernel, rows=64),
        grid=(bsz,),
        in_specs=[pl.BlockSpec((seq, 1), lambda b: (b, 0))],
        out_specs=[spec, spec, spec, spec],
        out_shape=[tab, tab, tab, tab],
        compiler_params=_cparams("parallel"),
        name="rope_tables",
    )(positions.reshape(bsz * seq, 1))


def _swap_attn(x, lane):
    h = DSA_ROPE // 2
    return jnp.where(lane < h, pltpu.roll(x, LANES - h, 1), pltpu.roll(x, h, 1))


def _swap_idx(x, lane_i):
    h = IDX_ROPE // 2
    return jnp.where(lane_i < h, pltpu.roll(x, LANES - h, 1), pltpu.roll(x, h, 1))


def _tree_sum(parts):
    while len(parts) > 1:
        parts = [parts[a] + parts[a + 1] for a in range(0, len(parts) - 1, 2)] + ([parts[-1]] if len(parts) % 2 else [])
    return parts[0]


def _dsa_kernel(q_ref, qi_ref, sm_ref, ca_ref, sa_ref, ci_ref, si_ref, o_ref,
                k_ref, kil_ref, kih_ref, v_ref, xr_ref, key_ref, bias_ref, qs_ref, l_ref,
                m_ref, s_ref, acc_ref, *, topk, tq, kc):
    qb = pl.program_id(1)
    seq = sm_ref.shape[0]
    lane = lax.broadcasted_iota(jnp.int32, (1, LANES), 1)
    lane_i = lane & (IDX_DIM - 1)

    @pl.when(qb == 0)
    def _prep():
        rows = min(256, seq)

        def body(r, carry):
            sl = pl.ds(pl.multiple_of(r * rows, rows), rows)
            k = sm_ref[sl, S_K:S_K + LANES]
            k_ref[sl, :] = (k * ca_ref[sl, :] + _swap_attn(k, lane) * sa_ref[sl, :]).astype(BF16)
            v_ref[sl, :] = sm_ref[sl, S_V:S_V + LANES].astype(BF16)
            ki = sm_ref[sl, S_KI:S_KI + LANES]
            kir = ki * ci_ref[sl, :] + _swap_idx(ki, lane_i) * si_ref[sl, :]
            kil = jnp.where(lane < IDX_DIM, kir, 0.0)
            kil_ref[sl, :] = kil.astype(BF16)
            kih_ref[sl, :] = pltpu.roll(kil, IDX_DIM, 1).astype(BF16)
            return carry

        lax.fori_loop(0, seq // rows, body, 0)

    row0 = pl.multiple_of(qb * tq, tq)
    qrows = pl.ds(row0, tq)
    ca_q, sa_q = ca_ref[qrows, :], sa_ref[qrows, :]
    ci_q, si_q = ci_ref[qrows, :], si_ref[qrows, :]

    n_chunks = (row0 + tq + kc - 1) // kc

    def chunk_rows(c):
        return pl.ds(pl.multiple_of(c * kc, kc), kc)

    w_t = sm_ref[qrows, S_WI:S_WI + LANES].T * (IDX_DIM ** -0.5 * IDX_HEADS ** -0.5)
    for pr in range(IDX_HEADS // 2):
        x = qi_ref[:, pr * LANES:(pr + 1) * LANES]
        xr_ref[pr] = (x * ci_q + _swap_idx(x, lane_i) * si_q).astype(BF16)
    qpos = row0 + lax.broadcasted_iota(jnp.int32, (kc, tq), 1)
    kpos0 = lax.broadcasted_iota(jnp.int32, (kc, tq), 0)

    def index_chunk(c, carry):
        ks = chunk_rows(c)
        kil, kih = kil_ref[ks, :], kih_ref[ks, :]
        score = jnp.zeros((kc, tq), F32)
        for pr in range(IDX_HEADS // 2):
            xr = xr_ref[pr]
            score = score + (jnp.maximum(_dot_nt(kil, xr), 0.0) * w_t[2 * pr:2 * pr + 1, :]
                             + jnp.maximum(_dot_nt(kih, xr), 0.0) * w_t[2 * pr + 1:2 * pr + 2, :])
        bits = lax.bitcast_convert_type(score, jnp.int32)
        skey = jnp.where(bits >= 0, bits, bits ^ jnp.int32(0x7FFFFFFF))
        key_ref[ks, :] = jnp.where(kpos0 + c * kc <= qpos, skey, jnp.int32(INT_MIN))
        return carry

    lax.fori_loop(0, n_chunks, index_chunk, 0)

    def search(i, thr):
        cand = thr ^ lax.shift_left(jnp.int32(1), 31 - i)

        def count_chunk(c, acc):
            hit = jnp.where(key_ref[chunk_rows(c), :] >= cand, 1.0, 0.0)
            return acc + _tree_sum([hit[j * 8:(j + 1) * 8, :] for j in range(kc // 8)])

        acc = lax.fori_loop(0, n_chunks, count_chunk, jnp.zeros((8, tq), F32))
        return jnp.where(jnp.sum(acc, axis=0, keepdims=True) >= float(topk), cand, thr)

    thr = lax.fori_loop(0, 32, search, jnp.full((1, tq), INT_MIN, jnp.int32))

    def bias_chunk(c, carry):
        keep = key_ref[chunk_rows(c), :] >= thr
        bias = jnp.where(keep, jnp.where(kpos0 + c * kc <= qpos, 0.0, -jnp.inf), -jnp.inf)
        for kb in range(kc // LANES):
            bias_ref[c, :, kb * LANES:(kb + 1) * LANES] = bias[kb * LANES:(kb + 1) * LANES, :].T
        return carry

    lax.fori_loop(0, n_chunks, bias_chunk, 0)

    for h in range(DSA_HEADS):
        x = q_ref[:, h * LANES:(h + 1) * LANES]
        qs_ref[h * tq:(h + 1) * tq, :] = (x * ca_q + _swap_attn(x, lane) * sa_q).astype(BF16)

    scale2 = DSA_HEAD_DIM ** -0.5 * math.log2(math.e)
    m_ref[...] = jnp.full_like(m_ref, -jnp.inf)
    s_ref[...] = jnp.zeros_like(s_ref)
    acc_ref[...] = jnp.zeros_like(acc_ref)

    def logits_chunk(c, carry):
        lg = _dot_nt(qs_ref[...], k_ref[chunk_rows(c), :]) * scale2
        lg = lg + jnp.concatenate([bias_ref[c]] * DSA_HEADS, axis=0)
        l_ref[c] = lg
        part = lg[:, 0:LANES]
        for kb in range(1, kc // LANES):
            part = jnp.maximum(part, lg[:, kb * LANES:(kb + 1) * LANES])
        m_ref[...] = jnp.maximum(m_ref[...], part)
        return carry

    lax.fori_loop(0, n_chunks, logits_chunk, 0)
    m_ref[...] = jnp.broadcast_to(jnp.max(m_ref[...], axis=-1, keepdims=True), m_ref.shape)

    def value_chunk(c, carry):
        e = jnp.exp2(l_ref[c] - jnp.concatenate([m_ref[...]] * (kc // LANES), axis=1))
        s_ref[...] += _tree_sum([e[:, kb * LANES:(kb + 1) * LANES] for kb in range(kc // LANES)])
        acc_ref[...] += _dot(e.astype(BF16), v_ref[chunk_rows(c), :])
        return carry

    lax.fori_loop(0, n_chunks, value_chunk, 0)
    for h in range(DSA_HEADS):
        hs = slice(h * tq, (h + 1) * tq)
        inv = 1.0 / jnp.sum(s_ref[hs, :], axis=-1, keepdims=True)
        o_ref[:, h * LANES:(h + 1) * LANES] = (acc_ref[hs, :] * inv).astype(o_ref.dtype)


def _dsa(proj, tables, *, bsz, seq):
    tq = DSA_QBLOCK
    kc = min(256, seq)
    topk = min(DSA_TOPK_MAX, seq // 4)
    nq = seq // tq
    tab_spec = pl.BlockSpec((seq, LANES), lambda b, i: (b, 0))
    return pl.pallas_call(
        functools.partial(_dsa_kernel, topk=topk, tq=tq, kc=kc),
        grid=(bsz, nq),
        in_specs=[pl.BlockSpec((tq, DSA_WIDTH), lambda b, i: (b * nq + i, P_Q // DSA_WIDTH)),
                  pl.BlockSpec((tq, IDX_HEADS * IDX_DIM), lambda b, i: (b * nq + i, P_QI // (IDX_HEADS * IDX_DIM))),
                  pl.BlockSpec((seq, 512), lambda b, i: (b, P_SMALL // 512)),
                  tab_spec, tab_spec, tab_spec, tab_spec],
        out_specs=pl.BlockSpec((tq, DSA_WIDTH), lambda b, i: (b * nq + i, 0)),
        out_shape=jax.ShapeDtypeStruct((bsz * seq, DSA_WIDTH), BF16),
        scratch_shapes=[pltpu.VMEM((seq, LANES), BF16),
                        pltpu.VMEM((seq, LANES), BF16),
                        pltpu.VMEM((seq, LANES), BF16),
                        pltpu.VMEM((seq, LANES), BF16),
                        pltpu.VMEM((IDX_HEADS // 2, tq, LANES), BF16),
                        pltpu.VMEM((seq, tq), jnp.int32),
                        pltpu.VMEM((seq // kc, tq, kc), F32),
                        pltpu.VMEM((DSA_HEADS * tq, LANES), BF16),
                        pltpu.VMEM((seq // kc, DSA_HEADS * tq, kc), F32),
                        pltpu.VMEM((DSA_HEADS * tq, LANES), F32),
                        pltpu.VMEM((DSA_HEADS * tq, LANES), F32),
                        pltpu.VMEM((DSA_HEADS * tq, LANES), F32)],
        compiler_params=_cparams("parallel", "arbitrary"),
        name="dsa",
    )(proj, proj, proj, *tables)


def _split_bf16(x, parts):
    out = []
    for _ in range(parts - 1):
        hi = x.astype(BF16)
        out.append(hi)
        x = x - hi.astype(F32)
    out.append(x.astype(BF16))
    return out


def _ssd_kernel(z_ref, xs_ref, bc_ref, sm_ref, cwx_ref, cwb_ref, cbx_ref, cbb_ref,
                dtb_ref, alog_ref, dskip_ref, ng_ref, e_ref, o_ref,
                xpad_ref, bpad_ref, state_ref, y_ref):
    c = pl.program_id(1)
    q = SSD_CHUNK
    pad = 8

    @pl.when(c == 0)
    def _():
        xpad_ref[0:pad, :] = jnp.zeros((pad, SSD_INNER), F32)
        bpad_ref[0:pad, :] = jnp.zeros((pad, SSD_BC), F32)
        state_ref[...] = jnp.zeros_like(state_ref)

    xpad_ref[pad:pad + q, :] = xs_ref[...]
    bpad_ref[pad:pad + q, :] = bc_ref[...]

    def conv_silu(p_ref, w_ref, b_ref):
        acc = b_ref[...] + w_ref[SSD_CONV - 1:SSD_CONV, :] * p_ref[pad:pad + q, :]
        for j in range(1, SSD_CONV):
            acc = acc + w_ref[SSD_CONV - 1 - j:SSD_CONV - j, :] * p_ref[pad - j:pad - j + q, :]
        return acc * _sigmoid(acc)

    xs = conv_silu(xpad_ref, cwx_ref, cbx_ref)
    bc = conv_silu(bpad_ref, cwb_ref, cbb_ref)
    xpad_ref[0:pad, :] = xs_ref[q - pad:q, :]
    bpad_ref[0:pad, :] = bc_ref[q - pad:q, :]

    pre = sm_ref[:, S_DT:S_DT + SSD_HEADS] + dtb_ref[...]
    dt = jnp.maximum(pre, 0.0) + jnp.log1p(jnp.exp(-jnp.abs(pre)))
    da = dt * (-jnp.exp(alog_ref[...]))
    li = lax.broadcasted_iota(jnp.int32, (q, q), 0)
    si = lax.broadcasted_iota(jnp.int32, (q, q), 1)
    causal = li >= si
    tri = jnp.where(causal, 1.0, 0.0).astype(BF16)
    a_cs = sum(_dot(tri, part) for part in _split_bf16(da, 3))
    a_cs_t = jnp.concatenate([a_cs, jnp.zeros_like(a_cs)], axis=1).T

    expand = e_ref[...]
    dt_x = sum(_dot(part, expand) for part in _split_bf16(dt, 2))
    acs_x = sum(_dot(part, expand) for part in _split_bf16(a_cs, 3))
    ea_x = jnp.exp(acs_x)
    dte_x = jnp.exp(acs_x[q - 1:q, :] - acs_x)
    xdt = xs * dt_x
    xdt_b = xdt.astype(BF16)
    xw_b = (xdt * dte_x).astype(BF16)
    cd_x = ea_x[q - 1:q, :]

    gw = SSD_GROUP_WIDTH
    for g in range(SSD_GROUPS):
        bm = bc[:, g * SSD_STATE:(g + 1) * SSD_STATE]
        cm_b = bc[:, (SSD_GROUPS + g) * SSD_STATE:(SSD_GROUPS + g + 1) * SSD_STATE].astype(BF16)
        cb = _dot_nt(cm_b, bm.astype(BF16))
        st = state_ref[g]
        y_ref[:, g * gw:(g + 1) * gw] = _dot(cm_b, st.astype(BF16)) * ea_x[:, g * gw:(g + 1) * gw]
        for r in range(SSD_HEADS_PER_GROUP):
            h = g * SSD_HEADS_PER_GROUP + r
            seg = a_cs[:, h:h + 1] - a_cs_t[h:h + 1, :]
            decay = jnp.exp(jnp.where(causal, seg, -jnp.inf))
            lo, hi = h * SSD_HEAD_DIM, (h + 1) * SSD_HEAD_DIM
            y_ref[:, lo:hi] += _dot((cb * decay).astype(BF16), xdt_b[:, lo:hi])
        upd = _dot(bm.T.astype(BF16), xw_b[:, g * gw:(g + 1) * gw])
        state_ref[g] = st * cd_x[:, g * gw:(g + 1) * gw] + upd

    z = z_ref[...]
    y = (y_ref[...] + xs * dskip_ref[...]) * (z * _sigmoid(z))
    ms = jnp.mean(y * y, axis=-1, keepdims=True)
    o_ref[...] = (y * lax.rsqrt(ms + NORM_EPS) * ng_ref[...]).astype(o_ref.dtype)


def _ssd(proj, conv_w, conv_b, dt_bias, a_log, d_skip, norm_g, *, bsz, seq):
    q = SSD_CHUNK
    nc = seq // q
    expand = jnp.asarray(np.repeat(np.eye(SSD_HEADS, dtype=np.float32), SSD_HEAD_DIM, axis=1), BF16)
    row = lambda b, c: (b * nc + c)
    const = lambda b, c: (0, 0)
    return pl.pallas_call(
        _ssd_kernel,
        grid=(bsz, nc),
        in_specs=[pl.BlockSpec((q, SSD_INNER), lambda b, c: (row(b, c), P_Z // SSD_INNER)),
                  pl.BlockSpec((q, SSD_INNER), lambda b, c: (row(b, c), P_XS // SSD_INNER)),
                  pl.BlockSpec((q, SSD_BC), lambda b, c: (row(b, c), P_BC // SSD_BC)),
                  pl.BlockSpec((q, 512), lambda b, c: (row(b, c), P_SMALL // 512)),
                  pl.BlockSpec((SSD_CONV, SSD_INNER), const),
                  pl.BlockSpec((SSD_CONV, SSD_BC), const),
                  pl.BlockSpec((1, SSD_INNER), const),
                  pl.BlockSpec((1, SSD_BC), const),
                  pl.BlockSpec((1, SSD_HEADS), const),
                  pl.BlockSpec((1, SSD_HEADS), const),
                  pl.BlockSpec((1, SSD_INNER), const),
                  pl.BlockSpec((1, SSD_INNER), const),
                  pl.BlockSpec((SSD_HEADS, SSD_INNER), const)],
        out_specs=pl.BlockSpec((q, SSD_INNER), lambda b, c: (row(b, c), 0)),
        out_shape=jax.ShapeDtypeStruct((bsz * seq, SSD_INNER), BF16),
        scratch_shapes=[pltpu.VMEM((q + 8, SSD_INNER), F32),
                        pltpu.VMEM((q + 8, SSD_BC), F32),
                        pltpu.VMEM((SSD_GROUPS, SSD_STATE, SSD_GROUP_WIDTH), F32),
                        pltpu.VMEM((q, SSD_INNER), F32)],
        compiler_params=_cparams("parallel", "arbitrary"),
        name="ssd",
    )(proj, proj, proj, proj,
      conv_w[:, :SSD_INNER], conv_w[:, SSD_INNER:], conv_b[None, :SSD_INNER], conv_b[None, SSD_INNER:],
      dt_bias[None, :], a_log[None, :], jnp.repeat(d_skip, SSD_HEAD_DIM)[None, :], norm_g[None, :], expand)


def _regroup_w_in(w_in):
    o = _IN_OFF
    sl = lambda a, b: w_in[:, :, a:b]
    z, xbc, dt, q, k, v, qi, ki, wi, qm, gate = (sl(o[i], o[i + 1]) for i in range(11))
    small_pad = jnp.zeros(w_in.shape[:2] + (512 - (S_WI + IDX_HEADS),), w_in.dtype)
    cols = [z, xbc[:, :, :SSD_INNER], q, qm, gate, xbc[:, :, SSD_INNER:], qi, k, v, ki, dt, wi, small_pad]
    out = jnp.concatenate(cols, axis=-1).astype(BF16)
    assert out.shape[-1] == P_WIDTH
    return out


def kernel(x, mem, positions, ffn1_norm, w_ffn1_in, w_ffn1_out, mix_norm, w_in, conv_w, conv_b, dt_bias, a_log,
           d_skip, ssd_norm, mem_norm, w_mem_kv, w_br_ssd, w_br_dsa, w_br_mem, w_out, ffn2_norm, w_ffn2_in,
           w_ffn2_out, final_norm):
    bsz, seq, d = x.shape
    depth = w_in.shape[0]
    assert d == D_MODEL and seq % DSA_QBLOCK == 0 and seq % SSD_CHUNK == 0 and mem.shape[1] == MEM_LEN

    w_p = _regroup_w_in(w_in)
    w1i, w1o = w_ffn1_in.astype(BF16), w_ffn1_out.astype(BF16)
    w2i, w2o = w_ffn2_in.astype(BF16), w_ffn2_out.astype(BF16)
    wkv = w_mem_kv.astype(BF16)
    wbs, wbd, wbm, wo = (w.astype(BF16) for w in (w_br_ssd, w_br_dsa, w_br_mem, w_out))

    tables = _rope_tables(positions)
    xf = x.reshape(bsz * seq, d)
    memf = mem.reshape(bsz * MEM_LEN, d)
    for l in range(depth):
        xf = _ffn(xf, ffn1_norm[l], w1i, w1o, final_norm, l, final_norm=False)
        proj = _norm_matmul(xf, mix_norm[l], w_p, l, tm=1024, tn=512, out_dtype=F32, name="in_proj")
        y_ssd = _ssd(proj, conv_w[l], conv_b[l], dt_bias[l], a_log[l], d_skip[l], ssd_norm[l], bsz=bsz, seq=seq)
        y_dsa = _dsa(proj, tables, bsz=bsz, seq=seq)
        kv = _norm_matmul(memf, mem_norm[l], wkv, l, tm=1024, tn=512, out_dtype=BF16, name="mem_kv")
        y_mem = _memattn(proj, kv, bsz=bsz, seq=seq)
        xf = _merge(xf, y_ssd, y_dsa, y_mem, proj, wbs, wbd, wbm, wo, l)
        xf = _ffn(xf, ffn2_norm[l], w2i, w2o, final_norm, l, final_norm=(l == depth - 1))
    return xf.reshape(bsz, seq, d)
```

```python
import functools
import math

import numpy as np
import jax
import jax.numpy as jnp
from jax import lax
from jax.experimental import pallas as pl
from jax.experimental.pallas import tpu as pltpu

F32 = jnp.float32
BF16 = jnp.bfloat16

D_MODEL = 2048
MEM_LEN = 256
NORM_EPS = 1e-6
ROPE_THETA = 500000.0
D_FF = 5632

SSD_INNER = 4096
SSD_HEAD_DIM = 64
SSD_HEADS = 64
SSD_GROUPS = 8
SSD_HEADS_PER_GROUP = 8
SSD_STATE = 128
SSD_CONV = 4
SSD_CHUNK = 128
SSD_GROUP_WIDTH = SSD_HEADS_PER_GROUP * SSD_HEAD_DIM
SSD_BC = 2 * SSD_GROUPS * SSD_STATE

DSA_HEADS = 16
DSA_HEAD_DIM = 128
DSA_WIDTH = 2048
DSA_ROPE = 32
IDX_HEADS = 16
IDX_DIM = 64
IDX_ROPE = 16
DSA_TOPK_MAX = 256
DSA_QBLOCK = 128

XA_HEADS = 4
XA_HEAD_DIM = 512
XA_WIDTH = 2048

_IN_SPLITS = (SSD_INNER, SSD_INNER + SSD_BC, SSD_HEADS, DSA_WIDTH, DSA_HEAD_DIM, DSA_HEAD_DIM,
              IDX_HEADS * IDX_DIM, IDX_DIM, IDX_HEADS, XA_WIDTH, 3 * D_MODEL)
_IN_OFF = np.concatenate([[0], np.cumsum(_IN_SPLITS)]).tolist()

P_Z = 0
P_XS = 4096
P_Q = 8192
P_QMEM = 10240
P_GATE = 12288
P_BC = 18432
P_QI = 20480
P_WIDTH = 21504
S_K, S_V, S_KI, S_DT, S_WI = 0, 128, 256, 320, 384
S_WIDTH = 512
_O = _IN_OFF
_WIDE_PIECES = ((_O[0], P_Z, SSD_INNER), (_O[1], P_XS, SSD_INNER), (_O[3], P_Q, DSA_WIDTH),
                (_O[9], P_QMEM, XA_WIDTH), (_O[10], P_GATE, 3 * D_MODEL), (_O[1] + SSD_INNER, P_BC, SSD_BC),
                (_O[6], P_QI, IDX_HEADS * IDX_DIM))
_NARROW_PIECES = ((_O[4], S_K, DSA_HEAD_DIM), (_O[5], S_V, DSA_HEAD_DIM), (_O[7], S_KI, IDX_DIM),
                  (_O[2], S_DT, SSD_HEADS), (_O[8], S_WI, IDX_HEADS))

LANES = 128
V7X_VMEM_LIMIT = 56 * 1024 * 1024
INT_MIN = -(2 ** 31)


def _cparams(*sem):
    return pltpu.CompilerParams(dimension_semantics=sem, vmem_limit_bytes=V7X_VMEM_LIMIT)


def _dot(a, b):
    return jnp.dot(a, b, preferred_element_type=F32)


def _dot_nt(a, b):
    return lax.dot_general(a, b, (((1,), (1,)), ((), ())), preferred_element_type=F32)


def _sigmoid(x):
    return 1.0 / (1.0 + jnp.exp(-x))


def _rms_rows_to(x_ref, g_ref, h_ref, rows):
    n = x_ref.shape[0] // rows

    def body(r, carry):
        sl = pl.ds(pl.multiple_of(r * rows, rows), rows)
        x = x_ref[sl, :]
        ms = jnp.mean(x * x, axis=-1, keepdims=True)
        h_ref[sl, :] = (x * lax.rsqrt(ms + NORM_EPS) * g_ref[...]).astype(h_ref.dtype)
        return carry

    lax.fori_loop(0, n, body, 0)


def _norm_matmul_kernel(x_ref, g_ref, w_ref, o_ref, h_ref):
    @pl.when(pl.program_id(1) == 0)
    def _():
        _rms_rows_to(x_ref, g_ref, h_ref, min(128, x_ref.shape[0]))

    o_ref[...] = _dot(h_ref[...], w_ref[...]).astype(o_ref.dtype)


def _norm_matmul(x, g, w, layer, *, tm, tn, out_dtype, name):
    m, d = x.shape
    n = w.shape[2]
    tm = min(tm, m)
    return pl.pallas_call(
        _norm_matmul_kernel,
        grid=(m // tm, n // tn),
        in_specs=[pl.BlockSpec((tm, d), lambda i, j: (i, 0)),
                  pl.BlockSpec((1, d), lambda i, j: (0, 0)),
                  pl.BlockSpec((None, d, tn), lambda i, j: (layer, 0, j))],
        out_specs=pl.BlockSpec((tm, tn), lambda i, j: (i, j)),
        out_shape=jax.ShapeDtypeStruct((m, n), out_dtype),
        scratch_shapes=[pltpu.VMEM((tm, d), BF16)],
        compiler_params=_cparams("parallel", "arbitrary"),
        name=name,
    )(x, g.reshape(1, d), w)


def _ffn_kernel(x_ref, g_ref, wg_ref, wu_ref, wo_ref, fg_ref, o_ref, h_ref, *, final_norm):
    j = pl.program_id(1)

    @pl.when(j == 0)
    def _():
        _rms_rows_to(x_ref, g_ref, h_ref, min(128, x_ref.shape[0]))
        o_ref[...] = x_ref[...]

    h = h_ref[...]
    gate = _dot(h, wg_ref[...])
    up = _dot(h, wu_ref[...])
    act = (gate * _sigmoid(gate) * up).astype(BF16)
    o_ref[...] += 0.5 * _dot(act, wo_ref[...])

    if final_norm:
        @pl.when(j == pl.num_programs(1) - 1)
        def _():
            _rms_rows_to(o_ref, fg_ref, o_ref, min(128, o_ref.shape[0]))


def _ffn(x, g, w_in, w_out, final_g, layer, *, final_norm, tm=512, tf=512):
    m, d = x.shape
    f = w_out.shape[1]
    tm = min(tm, m)
    nf = f // tf
    return pl.pallas_call(
        functools.partial(_ffn_kernel, final_norm=final_norm),
        grid=(m // tm, nf),
        in_specs=[pl.BlockSpec((tm, d), lambda i, j: (i, 0)),
                  pl.BlockSpec((1, d), lambda i, j: (0, 0)),
                  pl.BlockSpec((None, d, tf), lambda i, j: (layer, 0, j)),
                  pl.BlockSpec((None, d, tf), lambda i, j: (layer, 0, j + nf)),
                  pl.BlockSpec((None, tf, d), lambda i, j: (layer, j, 0)),
                  pl.BlockSpec((1, d), lambda i, j: (0, 0))],
        out_specs=pl.BlockSpec((tm, d), lambda i, j: (i, 0)),
        out_shape=jax.ShapeDtypeStruct((m, d), F32),
        scratch_shapes=[pltpu.VMEM((tm, d), BF16)],
        compiler_params=_cparams("parallel", "arbitrary"),
        name="ffn_final" if final_norm else "ffn",
    )(x, g.reshape(1, d), w_in, w_in, w_out, final_g.reshape(1, d))


def _branch_merge_kernel(ys_ref, yd_ref, ym_ref, g0_ref, g1_ref, g2_ref, ws_ref, wd_ref, wm_ref, o_ref):
    merged = (_sigmoid(g0_ref[...].astype(F32)) * _dot(ys_ref[...], ws_ref[...])
              + _sigmoid(g1_ref[...].astype(F32)) * _dot(yd_ref[...], wd_ref[...])
              + _sigmoid(g2_ref[...].astype(F32)) * _dot(ym_ref[...], wm_ref[...]))
    o_ref[...] = merged.astype(o_ref.dtype)


def _branch_merge(y_ssd, y_dsa, y_mem, proj, w_s, w_d, w_m, layer, *, tm=1024, tn=256):
    m = y_ssd.shape[0]
    d = D_MODEL
    tm = min(tm, m)
    gate_blk = P_GATE // tn
    per_branch = d // tn
    return pl.pallas_call(
        _branch_merge_kernel,
        grid=(m // tm, d // tn),
        in_specs=[pl.BlockSpec((tm, SSD_INNER), lambda i, j: (i, 0)),
                  pl.BlockSpec((tm, DSA_WIDTH), lambda i, j: (i, 0)),
                  pl.BlockSpec((tm, XA_WIDTH), lambda i, j: (i, 0)),
                  pl.BlockSpec((tm, tn), lambda i, j: (i, gate_blk + j)),
                  pl.BlockSpec((tm, tn), lambda i, j: (i, gate_blk + per_branch + j)),
                  pl.BlockSpec((tm, tn), lambda i, j: (i, gate_blk + 2 * per_branch + j)),
                  pl.BlockSpec((None, SSD_INNER, tn), lambda i, j: (layer, 0, j)),
                  pl.BlockSpec((None, DSA_WIDTH, tn), lambda i, j: (layer, 0, j)),
                  pl.BlockSpec((None, XA_WIDTH, tn), lambda i, j: (layer, 0, j))],
        out_specs=pl.BlockSpec((tm, tn), lambda i, j: (i, j)),
        out_shape=jax.ShapeDtypeStruct((m, d), BF16),
        compiler_params=_cparams("parallel", "parallel"),
        name="branch_merge",
    )(y_ssd, y_dsa, y_mem, proj, proj, proj, w_s, w_d, w_m)


def _out_proj_kernel(x_ref, m_ref, w_ref, o_ref):
    o_ref[...] = x_ref[...] + _dot(m_ref[...], w_ref[...])


def _out_proj(x, merged, w_o, layer, *, tm=512):
    m, d = x.shape
    tm = min(tm, m)
    return pl.pallas_call(
        _out_proj_kernel,
        grid=(m // tm,),
        in_specs=[pl.BlockSpec((tm, d), lambda i: (i, 0)),
                  pl.BlockSpec((tm, d), lambda i: (i, 0)),
                  pl.BlockSpec((None, d, d), lambda i: (layer, 0, 0))],
        out_specs=pl.BlockSpec((tm, d), lambda i: (i, 0)),
        out_shape=jax.ShapeDtypeStruct((m, d), F32),
        compiler_params=_cparams("parallel"),
        name="out_proj",
    )(x, merged, w_o)


def _memattn_kernel(q_ref, k_ref, v_ref, o_ref, *, rows):
    k = k_ref[...]
    v = v_ref[...]
    scale = XA_HEAD_DIM ** -0.5

    def body(r, carry):
        sl = pl.ds(pl.multiple_of(r * rows, rows), rows)
        q = q_ref[sl, :]
        logits = _dot_nt(q, k) * scale
        mx = jnp.max(logits, axis=-1, keepdims=True)
        e = jnp.exp(logits - mx)
        p = e * (1.0 / jnp.sum(e, axis=-1, keepdims=True))
        o_ref[sl, :] = _dot(p.astype(BF16), v).astype(o_ref.dtype)
        return carry

    lax.fori_loop(0, q_ref.shape[0] // rows, body, 0)


def _memattn(proj, kv, *, bsz, seq):
    qblk = P_QMEM // XA_HEAD_DIM
    return pl.pallas_call(
        functools.partial(_memattn_kernel, rows=min(256, seq)),
        grid=(bsz, XA_HEADS),
        in_specs=[pl.BlockSpec((seq, XA_HEAD_DIM), lambda b, h: (b, qblk + h)),
                  pl.BlockSpec((MEM_LEN, XA_HEAD_DIM), lambda b, h: (b, h)),
                  pl.BlockSpec((MEM_LEN, XA_HEAD_DIM), lambda b, h: (b, XA_HEADS + h))],
        out_specs=pl.BlockSpec((seq, XA_HEAD_DIM), lambda b, h: (b, h)),
        out_shape=jax.ShapeDtypeStruct((bsz * seq, XA_WIDTH), BF16),
        compiler_params=_cparams("parallel", "parallel"),
        name="memattn",
    )(proj, kv, kv)


def _rope_tables_kernel(pos_ref, ca_ref, sa_ref, ci_ref, si_ref, *, rows):
    lane = lax.broadcasted_iota(jnp.int32, (1, LANES), 1)
    ln_theta = math.log(ROPE_THETA)
    inv_a = jnp.exp((lane & (DSA_ROPE // 2 - 1)).astype(F32) * (-2.0 * ln_theta / DSA_ROPE))
    inv_i = jnp.exp((lane & (IDX_ROPE // 2 - 1)).astype(F32) * (-2.0 * ln_theta / IDX_ROPE))
    lane_i = lane & (IDX_DIM - 1)

    def body(r, carry):
        sl = pl.ds(pl.multiple_of(r * rows, rows), rows)
        pos = pos_ref[sl, :].astype(F32)
        ang = pos * inv_a
        c, s = jnp.cos(ang), jnp.sin(ang)
        ca_ref[sl, :] = jnp.where(lane < DSA_ROPE, c, 1.0)
        sa_ref[sl, :] = jnp.where(lane < DSA_ROPE // 2, -s, jnp.where(lane < DSA_ROPE, s, 0.0))
        ang = pos * inv_i
        c, s = jnp.cos(ang), jnp.sin(ang)
        ci_ref[sl, :] = jnp.where(lane_i < IDX_ROPE, c, 1.0)
        si_ref[sl, :] = jnp.where(lane_i < IDX_ROPE // 2, -s, jnp.where(lane_i < IDX_ROPE, s, 0.0))
        return carry

    lax.fori_loop(0, pos_ref.shape[0] // rows, body, 0)


def _rope_tables(positions):
    bsz, seq = positions.shape
    tab = jax.ShapeDtypeStruct((bsz * seq, LANES), F32)
    spec = pl.BlockSpec((seq, LANES), lambda b: (b, 0))
    return pl.pallas_call(
        functools.partial(_rope_tables_kernel, rows=64),
        grid=(bsz,),
        in_specs=[pl.BlockSpec((seq, 1), lambda b: (b, 0))],
        out_specs=[spec, spec, spec, spec],
        out_shape=[tab, tab, tab, tab],
        compiler_params=_cparams("parallel"),
        name="rope_tables",
    )(positions.reshape(bsz * seq, 1))


def _swap_attn(x, lane):
    h = DSA_ROPE // 2
    return jnp.where(lane < h, pltpu.roll(x, LANES - h, 1), pltpu.roll(x, h, 1))


def _swap_idx(x, lane_i):
    h = IDX_ROPE // 2
    return jnp.where(lane_i < h, pltpu.roll(x, LANES - h, 1), pltpu.roll(x, h, 1))


def _tree_sum(parts):
    while len(parts) > 1:
        parts = [parts[a] + parts[a + 1] for a in range(0, len(parts) - 1, 2)] + ([parts[-1]] if len(parts) % 2 else [])
    return parts[0]


def _dsa_kernel(q_ref, qi_ref, sm_ref, ca_ref, sa_ref, ci_ref, si_ref, o_ref,
                k_ref, kil_ref, kih_ref, v_ref, xr_ref, key_ref, bias_ref, qs_ref, l_ref,
                m_ref, s_ref, acc_ref, *, topk, tq, kc):
    qb = pl.program_id(1)
    seq = sm_ref.shape[0]
    lane = lax.broadcasted_iota(jnp.int32, (1, LANES), 1)
    lane_i = lane & (IDX_DIM - 1)

    @pl.when(qb == 0)
    def _prep():
        rows = min(256, seq)

        def body(r, carry):
            sl = pl.ds(pl.multiple_of(r * rows, rows), rows)
            k = sm_ref[sl, S_K:S_K + LANES]
            k_ref[sl, :] = (k * ca_ref[sl, :] + _swap_attn(k, lane) * sa_ref[sl, :]).astype(BF16)
            v_ref[sl, :] = sm_ref[sl, S_V:S_V + LANES].astype(BF16)
            ki = sm_ref[sl, S_KI:S_KI + LANES]
            kir = ki * ci_ref[sl, :] + _swap_idx(ki, lane_i) * si_ref[sl, :]
            kil = jnp.where(lane < IDX_DIM, kir, 0.0)
            kil_ref[sl, :] = kil.astype(BF16)
            kih_ref[sl, :] = pltpu.roll(kil, IDX_DIM, 1).astype(BF16)
            return carry

        lax.fori_loop(0, seq // rows, body, 0)

    row0 = pl.multiple_of(qb * tq, tq)
    qrows = pl.ds(row0, tq)
    ca_q, sa_q = ca_ref[qrows, :], sa_ref[qrows, :]
    ci_q, si_q = ci_ref[qrows, :], si_ref[qrows, :]

    n_chunks = (row0 + tq + kc - 1) // kc

    def chunk_rows(c):
        return pl.ds(pl.multiple_of(c * kc, kc), kc)

    w_t = sm_ref[qrows, S_WI:S_WI + LANES].T * (IDX_DIM ** -0.5 * IDX_HEADS ** -0.5)
    for pr in range(IDX_HEADS // 2):
        x = qi_ref[:, pr * LANES:(pr + 1) * LANES].astype(F32)
        xr_ref[pr] = (x * ci_q + _swap_idx(x, lane_i) * si_q).astype(BF16)
    qpos = row0 + lax.broadcasted_iota(jnp.int32, (kc, tq), 1)
    kpos0 = lax.broadcasted_iota(jnp.int32, (kc, tq), 0)

    def index_chunk(c, carry):
        ks = chunk_rows(c)
        kil, kih = kil_ref[ks, :], kih_ref[ks, :]
        score = jnp.zeros((kc, tq), F32)
        for pr in range(IDX_HEADS // 2):
            xr = xr_ref[pr]
            score = score + (jnp.maximum(_dot_nt(kil, xr), 0.0) * w_t[2 * pr:2 * pr + 1, :]
                             + jnp.maximum(_dot_nt(kih, xr), 0.0) * w_t[2 * pr + 1:2 * pr + 2, :])
        bits = lax.bitcast_convert_type(score, jnp.int32)
        skey = jnp.where(bits >= 0, bits, bits ^ jnp.int32(0x7FFFFFFF))
        key_ref[ks, :] = jnp.where(kpos0 + c * kc <= qpos, skey, jnp.int32(INT_MIN))
        return carry

    lax.fori_loop(0, n_chunks, index_chunk, 0)

    def search(i, thr):
        cand = thr ^ lax.shift_left(jnp.int32(1), 31 - i)

        def count_chunk(c, acc):
            hit = jnp.where(key_ref[chunk_rows(c), :] >= cand, 1.0, 0.0)
            return acc + _tree_sum([hit[j * 8:(j + 1) * 8, :] for j in range(kc // 8)])

        acc = lax.fori_loop(0, n_chunks, count_chunk, jnp.zeros((8, tq), F32))
        return jnp.where(jnp.sum(acc, axis=0, keepdims=True) >= float(topk), cand, thr)

    thr = lax.fori_loop(0, 32, search, jnp.full((1, tq), INT_MIN, jnp.int32))

    def bias_chunk(c, carry):
        keep = key_ref[chunk_rows(c), :] >= thr
        bias = jnp.where(keep, jnp.where(kpos0 + c * kc <= qpos, 0.0, -jnp.inf), -jnp.inf)
        for kb in range(kc // LANES):
            bias_ref[c, :, kb * LANES:(kb + 1) * LANES] = bias[kb * LANES:(kb + 1) * LANES, :].T
        return carry

    lax.fori_loop(0, n_chunks, bias_chunk, 0)

    for h in range(DSA_HEADS):
        x = q_ref[:, h * LANES:(h + 1) * LANES].astype(F32)
        qs_ref[h * tq:(h + 1) * tq, :] = (x * ca_q + _swap_attn(x, lane) * sa_q).astype(BF16)

    scale2 = DSA_HEAD_DIM ** -0.5 * math.log2(math.e)
    m_ref[...] = jnp.full_like(m_ref, -jnp.inf)
    s_ref[...] = jnp.zeros_like(s_ref)
    acc_ref[...] = jnp.zeros_like(acc_ref)

    def logits_chunk(c, carry):
        lg = _dot_nt(qs_ref[...], k_ref[chunk_rows(c), :]) * scale2
        lg = lg + jnp.concatenate([bias_ref[c]] * DSA_HEADS, axis=0)
        l_ref[c] = lg
        part = lg[:, 0:LANES]
        for kb in range(1, kc // LANES):
            part = jnp.maximum(part, lg[:, kb * LANES:(kb + 1) * LANES])
        m_ref[...] = jnp.maximum(m_ref[...], part)
        return carry

    lax.fori_loop(0, n_chunks, logits_chunk, 0)
    m_ref[...] = jnp.broadcast_to(jnp.max(m_ref[...], axis=-1, keepdims=True), m_ref.shape)

    def value_chunk(c, carry):
        e = jnp.exp2(l_ref[c] - jnp.concatenate([m_ref[...]] * (kc // LANES), axis=1))
        s_ref[...] += _tree_sum([e[:, kb * LANES:(kb + 1) * LANES] for kb in range(kc // LANES)])
        acc_ref[...] += _dot(e.astype(BF16), v_ref[chunk_rows(c), :])
        return carry

    lax.fori_loop(0, n_chunks, value_chunk, 0)
    for h in range(DSA_HEADS):
        hs = slice(h * tq, (h + 1) * tq)
        inv = 1.0 / jnp.sum(s_ref[hs, :], axis=-1, keepdims=True)
        o_ref[:, h * LANES:(h + 1) * LANES] = (acc_ref[hs, :] * inv).astype(o_ref.dtype)


def _dsa(proj, narrow, tables, *, bsz, seq):
    tq = DSA_QBLOCK
    kc = min(256, seq)
    topk = min(DSA_TOPK_MAX, seq // 4)
    nq = seq // tq
    tab_spec = pl.BlockSpec((seq, LANES), lambda b, i: (b, 0))
    return pl.pallas_call(
        functools.partial(_dsa_kernel, topk=topk, tq=tq, kc=kc),
        grid=(bsz, nq),
        in_specs=[pl.BlockSpec((tq, DSA_WIDTH), lambda b, i: (b * nq + i, P_Q // DSA_WIDTH)),
                  pl.BlockSpec((tq, IDX_HEADS * IDX_DIM), lambda b, i: (b * nq + i, P_QI // (IDX_HEADS * IDX_DIM))),
                  pl.BlockSpec((seq, S_WIDTH), lambda b, i: (b, 0)),
                  tab_spec, tab_spec, tab_spec, tab_spec],
        out_specs=pl.BlockSpec((tq, DSA_WIDTH), lambda b, i: (b * nq + i, 0)),
        out_shape=jax.ShapeDtypeStruct((bsz * seq, DSA_WIDTH), BF16),
        scratch_shapes=[pltpu.VMEM((seq, LANES), BF16),
                        pltpu.VMEM((seq, LANES), BF16),
                        pltpu.VMEM((seq, LANES), BF16),
                        pltpu.VMEM((seq, LANES), BF16),
                        pltpu.VMEM((IDX_HEADS // 2, tq, LANES), BF16),
                        pltpu.VMEM((seq, tq), jnp.int32),
                        pltpu.VMEM((seq // kc, tq, kc), F32),
                        pltpu.VMEM((DSA_HEADS * tq, LANES), BF16),
                        pltpu.VMEM((seq // kc, DSA_HEADS * tq, kc), F32),
                        pltpu.VMEM((DSA_HEADS * tq, LANES), F32),
                        pltpu.VMEM((DSA_HEADS * tq, LANES), F32),
                        pltpu.VMEM((DSA_HEADS * tq, LANES), F32)],
        compiler_params=_cparams("parallel", "arbitrary"),
        name="dsa",
    )(proj, proj, narrow, *tables)


def _split_bf16(x, parts):
    out = []
    for _ in range(parts - 1):
        hi = x.astype(BF16)
        out.append(hi)
        x = x - hi.astype(F32)
    out.append(x.astype(BF16))
    return out


def _ssd_kernel(z_ref, xs_ref, bc_ref, sm_ref, cwx_ref, cwb_ref, cbx_ref, cbb_ref,
                dtb_ref, alog_ref, dskip_ref, ng_ref, e_ref, o_ref,
                xpad_ref, bpad_ref, state_ref, y_ref):
    c = pl.program_id(1)
    q = SSD_CHUNK
    pad = 8

    @pl.when(c == 0)
    def _():
        xpad_ref[0:pad, :] = jnp.zeros((pad, SSD_INNER), F32)
        bpad_ref[0:pad, :] = jnp.zeros((pad, SSD_BC), F32)
        state_ref[...] = jnp.zeros_like(state_ref)

    xpad_ref[pad:pad + q, :] = xs_ref[...].astype(F32)
    bpad_ref[pad:pad + q, :] = bc_ref[...].astype(F32)

    def conv_silu(p_ref, w_ref, b_ref):
        acc = b_ref[...] + w_ref[SSD_CONV - 1:SSD_CONV, :] * p_ref[pad:pad + q, :]
        for j in range(1, SSD_CONV):
            acc = acc + w_ref[SSD_CONV - 1 - j:SSD_CONV - j, :] * p_ref[pad - j:pad - j + q, :]
        return acc * _sigmoid(acc)

    xs = conv_silu(xpad_ref, cwx_ref, cbx_ref)
    bc = conv_silu(bpad_ref, cwb_ref, cbb_ref)
    xpad_ref[0:pad, :] = xpad_ref[q:q + pad, :]
    bpad_ref[0:pad, :] = bpad_ref[q:q + pad, :]

    pre = sm_ref[:, S_DT:S_DT + SSD_HEADS] + dtb_ref[...]
    dt = jnp.maximum(pre, 0.0) + jnp.log1p(jnp.exp(-jnp.abs(pre)))
    da = dt * (-jnp.exp(alog_ref[...]))
    li = lax.broadcasted_iota(jnp.int32, (q, q), 0)
    si = lax.broadcasted_iota(jnp.int32, (q, q), 1)
    causal = li >= si
    tri = jnp.where(causal, 1.0, 0.0).astype(BF16)
    a_cs = sum(_dot(tri, part) for part in _split_bf16(da, 3))
    a_cs_t = jnp.concatenate([a_cs, jnp.zeros_like(a_cs)], axis=1).T

    expand = e_ref[...]
    dt_x = sum(_dot(part, expand) for part in _split_bf16(dt, 2))
    acs_x = sum(_dot(part, expand) for part in _split_bf16(a_cs, 3))
    ea_x = jnp.exp(acs_x)
    dte_x = jnp.exp(acs_x[q - 1:q, :] - acs_x)
    xdt = xs * dt_x
    xdt_b = xdt.astype(BF16)
    xw_b = (xdt * dte_x).astype(BF16)
    cd_x = ea_x[q - 1:q, :]

    gw = SSD_GROUP_WIDTH
    for g in range(SSD_GROUPS):
        bm = bc[:, g * SSD_STATE:(g + 1) * SSD_STATE]
        cm_b = bc[:, (SSD_GROUPS + g) * SSD_STATE:(SSD_GROUPS + g + 1) * SSD_STATE].astype(BF16)
        cb = _dot_nt(cm_b, bm.astype(BF16))
        st = state_ref[g]
        y_ref[:, g * gw:(g + 1) * gw] = _dot(cm_b, st.astype(BF16)) * ea_x[:, g * gw:(g + 1) * gw]
        for r in range(SSD_HEADS_PER_GROUP):
            h = g * SSD_HEADS_PER_GROUP + r
            seg = a_cs[:, h:h + 1] - a_cs_t[h:h + 1, :]
            decay = jnp.exp(jnp.where(causal, seg, -jnp.inf))
            lo, hi = h * SSD_HEAD_DIM, (h + 1) * SSD_HEAD_DIM
            y_ref[:, lo:hi] += _dot((cb * decay).astype(BF16), xdt_b[:, lo:hi])
        upd = _dot(bm.T.astype(BF16), xw_b[:, g * gw:(g + 1) * gw])
        state_ref[g] = st * cd_x[:, g * gw:(g + 1) * gw] + upd

    z = z_ref[...].astype(F32)
    y = (y_ref[...] + xs * dskip_ref[...]) * (z * _sigmoid(z))
    ms = jnp.mean(y * y, axis=-1, keepdims=True)
    o_ref[...] = (y * lax.rsqrt(ms + NORM_EPS) * ng_ref[...]).astype(o_ref.dtype)


def _ssd(proj, narrow, conv_w, conv_b, dt_bias, a_log, d_skip, norm_g, *, bsz, seq):
    q = SSD_CHUNK
    nc = seq // q
    expand = jnp.asarray(np.repeat(np.eye(SSD_HEADS, dtype=np.float32), SSD_HEAD_DIM, axis=1), BF16)
    row = lambda b, c: (b * nc + c)
    const = lambda b, c: (0, 0)
    return pl.pallas_call(
        _ssd_kernel,
        grid=(bsz, nc),
        in_specs=[pl.BlockSpec((q, SSD_INNER), lambda b, c: (row(b, c), P_Z // SSD_INNER)),
                  pl.BlockSpec((q, SSD_INNER), lambda b, c: (row(b, c), P_XS // SSD_INNER)),
                  pl.BlockSpec((q, SSD_BC), lambda b, c: (row(b, c), P_BC // SSD_BC)),
                  pl.BlockSpec((q, S_WIDTH), lambda b, c: (row(b, c), 0)),
                  pl.BlockSpec((SSD_CONV, SSD_INNER), const),
                  pl.BlockSpec((SSD_CONV, SSD_BC), const),
                  pl.BlockSpec((1, SSD_INNER), const),
                  pl.BlockSpec((1, SSD_BC), const),
                  pl.BlockSpec((1, SSD_HEADS), const),
                  pl.BlockSpec((1, SSD_HEADS), const),
                  pl.BlockSpec((1, SSD_INNER), const),
                  pl.BlockSpec((1, SSD_INNER), const),
                  pl.BlockSpec((SSD_HEADS, SSD_INNER), const)],
        out_specs=pl.BlockSpec((q, SSD_INNER), lambda b, c: (row(b, c), 0)),
        out_shape=jax.ShapeDtypeStruct((bsz * seq, SSD_INNER), BF16),
        scratch_shapes=[pltpu.VMEM((q + 8, SSD_INNER), F32),
                        pltpu.VMEM((q + 8, SSD_BC), F32),
                        pltpu.VMEM((SSD_GROUPS, SSD_STATE, SSD_GROUP_WIDTH), F32),
                        pltpu.VMEM((q, SSD_INNER), F32)],
        compiler_params=_cparams("parallel", "arbitrary"),
        name="ssd",
    )(proj, proj, proj, narrow,
      conv_w[:, :SSD_INNER], conv_w[:, SSD_INNER:], conv_b[None, :SSD_INNER], conv_b[None, SSD_INNER:],
      dt_bias[None, :], a_log[None, :], jnp.repeat(d_skip, SSD_HEAD_DIM)[None, :], norm_g[None, :], expand)


def _regroup_kernel(w_ref, wide_ref, narrow_ref, *, step):
    def move(dst_ref, src, dst, width):
        for c in range(0, width, step):
            n = min(step, width - c)
            dst_ref[:, dst + c:dst + c + n] = w_ref[:, src + c:src + c + n].astype(dst_ref.dtype)

    for src, dst, width in _WIDE_PIECES:
        move(wide_ref, src, dst, width)
    used = S_WI + IDX_HEADS
    narrow_ref[:, used:] = jnp.zeros((narrow_ref.shape[0], S_WIDTH - used), narrow_ref.dtype)
    for src, dst, width in _NARROW_PIECES:
        move(narrow_ref, src, dst, width)


def _regroup_w_in(w_in, *, rows=64):
    depth, d, n = w_in.shape
    return pl.pallas_call(
        functools.partial(_regroup_kernel, step=1024),
        grid=(depth, d // rows),
        in_specs=[pl.BlockSpec((None, rows, n), lambda l, r: (l, r, 0))],
        out_specs=[pl.BlockSpec((None, rows, P_WIDTH), lambda l, r: (l, r, 0)),
                   pl.BlockSpec((None, rows, S_WIDTH), lambda l, r: (l, r, 0))],
        out_shape=[jax.ShapeDtypeStruct((depth, d, P_WIDTH), BF16),
                   jax.ShapeDtypeStruct((depth, d, S_WIDTH), BF16)],
        compiler_params=_cparams("parallel", "parallel"),
        name="regroup_w_in",
    )(w_in)


def kernel(x, mem, positions, ffn1_norm, w_ffn1_in, w_ffn1_out, mix_norm, w_in, conv_w, conv_b, dt_bias, a_log,
           d_skip, ssd_norm, mem_norm, w_mem_kv, w_br_ssd, w_br_dsa, w_br_mem, w_out, ffn2_norm, w_ffn2_in,
           w_ffn2_out, final_norm):
    bsz, seq, d = x.shape
    depth = w_in.shape[0]
    assert d == D_MODEL and seq % DSA_QBLOCK == 0 and seq % SSD_CHUNK == 0 and mem.shape[1] == MEM_LEN

    w_p, w_n = _regroup_w_in(w_in)
    w1i, w1o = w_ffn1_in.astype(BF16), w_ffn1_out.astype(BF16)
    w2i, w2o = w_ffn2_in.astype(BF16), w_ffn2_out.astype(BF16)
    wkv = w_mem_kv.astype(BF16)
    wbs, wbd, wbm, wo = (w.astype(BF16) for w in (w_br_ssd, w_br_dsa, w_br_mem, w_out))

    tables = _rope_tables(positions)
    xf = x.reshape(bsz * seq, d)
    memf = mem.reshape(bsz * MEM_LEN, d)
    for l in range(depth):
        xf = _ffn(xf, ffn1_norm[l], w1i, w1o, final_norm, l, final_norm=False)
        proj = _norm_matmul(xf, mix_norm[l], w_p, l, tm=1024, tn=1024, out_dtype=BF16, name="in_proj")
        narrow = _norm_matmul(xf, mix_norm[l], w_n, l, tm=1024, tn=S_WIDTH, out_dtype=F32, name="in_proj_narrow")
        y_ssd = _ssd(proj, narrow, conv_w[l], conv_b[l], dt_bias[l], a_log[l], d_skip[l], ssd_norm[l],
                     bsz=bsz, seq=seq)
        y_dsa = _dsa(proj, narrow, tables, bsz=bsz, seq=seq)
        kv = _norm_matmul(memf, mem_norm[l], wkv, l, tm=1024, tn=512, out_dtype=BF16, name="mem_kv")
        y_mem = _memattn(proj, kv, bsz=bsz, seq=seq)
        merged = _branch_merge(y_ssd, y_dsa, y_mem, proj, wbs, wbd, wbm, l)
        xf = _out_proj(xf, merged, wo, l)
        xf = _ffn(xf, ffn2_norm[l], w2i, w2o, final_norm, l, final_norm=(l == depth - 1))
    return xf.reshape(bsz, seq, d)
```

```python
import functools
import math

import numpy as np
import jax
import jax.numpy as jnp
from jax import lax
from jax.experimental import pallas as pl
from jax.experimental.pallas import tpu as pltpu

F32 = jnp.float32
BF16 = jnp.bfloat16

D_MODEL = 2048
MEM_LEN = 256
NORM_EPS = 1e-6
ROPE_THETA = 500000.0
D_FF = 5632

SSD_INNER = 4096
SSD_HEAD_DIM = 64
SSD_HEADS = 64
SSD_GROUPS = 8
SSD_HEADS_PER_GROUP = 8
SSD_STATE = 128
SSD_CONV = 4
SSD_CHUNK = 128
SSD_GROUP_WIDTH = SSD_HEADS_PER_GROUP * SSD_HEAD_DIM
SSD_BC = 2 * SSD_GROUPS * SSD_STATE
SSD_TAIL = 16
SSD_STAGE_ROWS = 256

DSA_HEADS = 16
DSA_HEAD_DIM = 128
DSA_WIDTH = 2048
DSA_ROPE = 32
IDX_HEADS = 16
IDX_DIM = 64
IDX_ROPE = 16
DSA_TOPK_MAX = 256
DSA_QBLOCK = 128

XA_HEADS = 4
XA_HEAD_DIM = 512
XA_WIDTH = 2048

_IN_SPLITS = (SSD_INNER, SSD_INNER + SSD_BC, SSD_HEADS, DSA_WIDTH, DSA_HEAD_DIM, DSA_HEAD_DIM,
              IDX_HEADS * IDX_DIM, IDX_DIM, IDX_HEADS, XA_WIDTH, 3 * D_MODEL)
_IN_OFF = np.concatenate([[0], np.cumsum(_IN_SPLITS)]).tolist()

P_Z = 0
P_XS = 4096
P_Q = 8192
P_QMEM = 10240
P_GATE = 12288
P_BC = 18432
P_QI = 20480
P_WIDTH = 21504
S_K, S_V, S_KI, S_DT, S_WI = 0, 128, 256, 320, 384
S_WIDTH = 512
_O = _IN_OFF
_WIDE_PIECES = ((_O[0], P_Z, SSD_INNER), (_O[1], P_XS, SSD_INNER), (_O[3], P_Q, DSA_WIDTH),
                (_O[9], P_QMEM, XA_WIDTH), (_O[10], P_GATE, 3 * D_MODEL), (_O[1] + SSD_INNER, P_BC, SSD_BC),
                (_O[6], P_QI, IDX_HEADS * IDX_DIM))
_NARROW_PIECES = ((_O[4], S_K, DSA_HEAD_DIM), (_O[5], S_V, DSA_HEAD_DIM), (_O[7], S_KI, IDX_DIM),
                  (_O[2], S_DT, SSD_HEADS), (_O[8], S_WI, IDX_HEADS))

LANES = 128
V7X_VMEM_LIMIT = 56 * 1024 * 1024
INT_MIN = -(2 ** 31)


def _cparams(*sem):
    return pltpu.CompilerParams(dimension_semantics=sem, vmem_limit_bytes=V7X_VMEM_LIMIT)


def _dot(a, b):
    return jnp.dot(a, b, preferred_element_type=F32)


def _dot_nt(a, b):
    return lax.dot_general(a, b, (((1,), (1,)), ((), ())), preferred_element_type=F32)


def _sigmoid(x):
    return 1.0 / (1.0 + jnp.exp(-x))


def _rms_rows_to(x_ref, g_ref, h_ref, rows):
    n = x_ref.shape[0] // rows

    def body(r, carry):
        sl = pl.ds(pl.multiple_of(r * rows, rows), rows)
        x = x_ref[sl, :]
        ms = jnp.mean(x * x, axis=-1, keepdims=True)
        h_ref[sl, :] = (x * lax.rsqrt(ms + NORM_EPS) * g_ref[...]).astype(h_ref.dtype)
        return carry

    lax.fori_loop(0, n, body, 0)


def _norm_matmul_kernel(x_ref, g_ref, w_ref, o_ref, h_ref, *, w_is_transposed):
    @pl.when(pl.program_id(1) == 0)
    def _():
        _rms_rows_to(x_ref, g_ref, h_ref, min(128, x_ref.shape[0]))

    mm = _dot_nt if w_is_transposed else _dot
    o_ref[...] = mm(h_ref[...], w_ref[...]).astype(o_ref.dtype)


def _norm_matmul(x, g, w, layer, *, tm, tn, out_dtype, name, w_is_transposed=False):
    m, d = x.shape
    n = w.shape[1] if w_is_transposed else w.shape[2]
    tm = min(tm, m)
    if w_is_transposed:
        w_spec = pl.BlockSpec((None, tn, d), lambda i, j: (layer, j, 0))
    else:
        w_spec = pl.BlockSpec((None, d, tn), lambda i, j: (layer, 0, j))
    return pl.pallas_call(
        functools.partial(_norm_matmul_kernel, w_is_transposed=w_is_transposed),
        grid=(m // tm, n // tn),
        in_specs=[pl.BlockSpec((tm, d), lambda i, j: (i, 0)),
                  pl.BlockSpec((1, d), lambda i, j: (0, 0)),
                  w_spec],
        out_specs=pl.BlockSpec((tm, tn), lambda i, j: (i, j)),
        out_shape=jax.ShapeDtypeStruct((m, n), out_dtype),
        scratch_shapes=[pltpu.VMEM((tm, d), BF16)],
        compiler_params=_cparams("parallel", "arbitrary"),
        name=name,
    )(x, g.reshape(1, d), w)


def _ffn_kernel(x_ref, g_ref, wg_ref, wu_ref, wo_ref, fg_ref, o_ref, h_ref, *, final_norm):
    j = pl.program_id(1)

    @pl.when(j == 0)
    def _():
        _rms_rows_to(x_ref, g_ref, h_ref, min(128, x_ref.shape[0]))
        o_ref[...] = x_ref[...]

    h = h_ref[...]
    gate = _dot(h, wg_ref[...])
    up = _dot(h, wu_ref[...])
    act = (gate * _sigmoid(gate) * up).astype(BF16)
    o_ref[...] += 0.5 * _dot(act, wo_ref[...])

    if final_norm:
        @pl.when(j == pl.num_programs(1) - 1)
        def _():
            _rms_rows_to(o_ref, fg_ref, o_ref, min(128, o_ref.shape[0]))


def _ffn(x, g, w_in, w_out, final_g, layer, *, final_norm, tm=1024, tf=512):
    m, d = x.shape
    f = w_out.shape[1]
    tm = min(tm, m)
    nf = f // tf
    return pl.pallas_call(
        functools.partial(_ffn_kernel, final_norm=final_norm),
        grid=(m // tm, nf),
        in_specs=[pl.BlockSpec((tm, d), lambda i, j: (i, 0)),
                  pl.BlockSpec((1, d), lambda i, j: (0, 0)),
                  pl.BlockSpec((None, d, tf), lambda i, j: (layer, 0, j)),
                  pl.BlockSpec((None, d, tf), lambda i, j: (layer, 0, j + nf)),
                  pl.BlockSpec((None, tf, d), lambda i, j: (layer, j, 0)),
                  pl.BlockSpec((1, d), lambda i, j: (0, 0))],
        out_specs=pl.BlockSpec((tm, d), lambda i, j: (i, 0)),
        out_shape=jax.ShapeDtypeStruct((m, d), F32),
        scratch_shapes=[pltpu.VMEM((tm, d), BF16)],
        compiler_params=_cparams("parallel", "arbitrary"),
        name="ffn_final" if final_norm else "ffn",
    )(x, g.reshape(1, d), w_in, w_in, w_out, final_g.reshape(1, d))


def _branch_merge_kernel(ys_ref, yd_ref, ym_ref, g0_ref, g1_ref, g2_ref, ws_ref, wd_ref, wm_ref, o_ref):
    merged = (_sigmoid(g0_ref[...].astype(F32)) * _dot(ys_ref[...], ws_ref[...])
              + _sigmoid(g1_ref[...].astype(F32)) * _dot(yd_ref[...], wd_ref[...])
              + _sigmoid(g2_ref[...].astype(F32)) * _dot(ym_ref[...], wm_ref[...]))
    o_ref[...] = merged.astype(o_ref.dtype)


def _branch_merge(y_ssd, y_dsa, y_mem, proj, w_s, w_d, w_m, layer, *, tm=1024, tn=256):
    m = y_ssd.shape[0]
    d = D_MODEL
    tm = min(tm, m)
    gate_blk = P_GATE // tn
    per_branch = d // tn
    return pl.pallas_call(
        _branch_merge_kernel,
        grid=(m // tm, d // tn),
        in_specs=[pl.BlockSpec((tm, SSD_INNER), lambda i, j: (i, 0)),
                  pl.BlockSpec((tm, DSA_WIDTH), lambda i, j: (i, 0)),
                  pl.BlockSpec((tm, XA_WIDTH), lambda i, j: (i, 0)),
                  pl.BlockSpec((tm, tn), lambda i, j: (i, gate_blk + j)),
                  pl.BlockSpec((tm, tn), lambda i, j: (i, gate_blk + per_branch + j)),
                  pl.BlockSpec((tm, tn), lambda i, j: (i, gate_blk + 2 * per_branch + j)),
                  pl.BlockSpec((None, SSD_INNER, tn), lambda i, j: (layer, 0, j)),
                  pl.BlockSpec((None, DSA_WIDTH, tn), lambda i, j: (layer, 0, j)),
                  pl.BlockSpec((None, XA_WIDTH, tn), lambda i, j: (layer, 0, j))],
        out_specs=pl.BlockSpec((tm, tn), lambda i, j: (i, j)),
        out_shape=jax.ShapeDtypeStruct((m, d), BF16),
        compiler_params=_cparams("parallel", "parallel"),
        name="branch_merge",
    )(y_ssd, y_dsa, y_mem, proj, proj, proj, w_s, w_d, w_m)


def _out_proj_kernel(x_ref, m_ref, w_ref, o_ref):
    o_ref[...] = x_ref[...] + _dot(m_ref[...], w_ref[...])


def _out_proj(x, merged, w_o, layer, *, tm=512):
    m, d = x.shape
    tm = min(tm, m)
    return pl.pallas_call(
        _out_proj_kernel,
        grid=(m // tm,),
        in_specs=[pl.BlockSpec((tm, d), lambda i: (i, 0)),
                  pl.BlockSpec((tm, d), lambda i: (i, 0)),
                  pl.BlockSpec((None, d, d), lambda i: (layer, 0, 0))],
        out_specs=pl.BlockSpec((tm, d), lambda i: (i, 0)),
        out_shape=jax.ShapeDtypeStruct((m, d), F32),
        compiler_params=_cparams("parallel"),
        name="out_proj",
    )(x, merged, w_o)


def _memattn_kernel(q_ref, k_ref, v_ref, o_ref, *, rows):
    k = k_ref[...]
    v = v_ref[...]
    scale = XA_HEAD_DIM ** -0.5

    def body(r, carry):
        sl = pl.ds(pl.multiple_of(r * rows, rows), rows)
        q = q_ref[sl, :]
        logits = _dot_nt(q, k) * scale
        mx = jnp.max(logits, axis=-1, keepdims=True)
        e = jnp.exp(logits - mx)
        p = e * (1.0 / jnp.sum(e, axis=-1, keepdims=True))
        o_ref[sl, :] = _dot(p.astype(BF16), v).astype(o_ref.dtype)
        return carry

    lax.fori_loop(0, q_ref.shape[0] // rows, body, 0)


def _memattn(proj, kv, *, bsz, seq):
    qblk = P_QMEM // XA_HEAD_DIM
    return pl.pallas_call(
        functools.partial(_memattn_kernel, rows=min(256, seq)),
        grid=(bsz, XA_HEADS),
        in_specs=[pl.BlockSpec((seq, XA_HEAD_DIM), lambda b, h: (b, qblk + h)),
                  pl.BlockSpec((MEM_LEN, XA_HEAD_DIM), lambda b, h: (b, h)),
                  pl.BlockSpec((MEM_LEN, XA_HEAD_DIM), lambda b, h: (b, XA_HEADS + h))],
        out_specs=pl.BlockSpec((seq, XA_HEAD_DIM), lambda b, h: (b, h)),
        out_shape=jax.ShapeDtypeStruct((bsz * seq, XA_WIDTH), BF16),
        compiler_params=_cparams("parallel", "parallel"),
        name="memattn",
    )(proj, kv, kv)


def _rope_tables_kernel(pos_ref, ca_ref, sa_ref, ci_ref, si_ref, *, rows):
    lane = lax.broadcasted_iota(jnp.int32, (1, LANES), 1)
    ln_theta = math.log(ROPE_THETA)
    inv_a = jnp.exp((lane & (DSA_ROPE // 2 - 1)).astype(F32) * (-2.0 * ln_theta / DSA_ROPE))
    inv_i = jnp.exp((lane & (IDX_ROPE // 2 - 1)).astype(F32) * (-2.0 * ln_theta / IDX_ROPE))
    lane_i = lane & (IDX_DIM - 1)

    def body(r, carry):
        sl = pl.ds(pl.multiple_of(r * rows, rows), rows)
        pos = pos_ref[sl, :].astype(F32)
        ang = pos * inv_a
        c, s = jnp.cos(ang), jnp.sin(ang)
        ca_ref[sl, :] = jnp.where(lane < DSA_ROPE, c, 1.0)
        sa_ref[sl, :] = jnp.where(lane < DSA_ROPE // 2, -s, jnp.where(lane < DSA_ROPE, s, 0.0))
        ang = pos * inv_i
        c, s = jnp.cos(ang), jnp.sin(ang)
        ci_ref[sl, :] = jnp.where(lane_i < IDX_ROPE, c, 1.0)
        si_ref[sl, :] = jnp.where(lane_i < IDX_ROPE // 2, -s, jnp.where(lane_i < IDX_ROPE, s, 0.0))
        return carry

    lax.fori_loop(0, pos_ref.shape[0] // rows, body, 0)


def _rope_tables(positions):
    bsz, seq = positions.shape
    tab = jax.ShapeDtypeStruct((bsz * seq, LANES), F32)
    spec = pl.BlockSpec((seq, LANES), lambda b: (b, 0))
    return pl.pallas_call(
        functools.partial(_rope_tables_kernel, rows=64),
        grid=(bsz,),
        in_specs=[pl.BlockSpec((seq, 1), lambda b: (b, 0))],
        out_specs=[spec, spec, spec, spec],
        out_shape=[tab, tab, tab, tab],
        compiler_params=_cparams("parallel"),
        name="rope_tables",
    )(positions.reshape(bsz * seq, 1))


def _swap_attn(x, lane):
    h = DSA_ROPE // 2
    return jnp.where(lane < h, pltpu.roll(x, LANES - h, 1), pltpu.roll(x, h, 1))


def _swap_idx(x, lane_i):
    h = IDX_ROPE // 2
    return jnp.where(lane_i < h, pltpu.roll(x, LANES - h, 1), pltpu.roll(x, h, 1))


def _tree_sum(parts):
    while len(parts) > 1:
        parts = [parts[a] + parts[a + 1] for a in range(0, len(parts) - 1, 2)] + ([parts[-1]] if len(parts) % 2 else [])
    return parts[0]


def _dsa_kernel(q_ref, qi_ref, sm_ref, ca_ref, sa_ref, ci_ref, si_ref, o_ref,
                k_ref, kil_ref, kih_ref, v_ref, xr_ref, key_ref, bias_ref, qs_ref, l_ref,
                m_ref, s_ref, acc_ref, *, topk, tq, kc):
    qb = pl.program_id(1)
    seq = sm_ref.shape[0]
    lane = lax.broadcasted_iota(jnp.int32, (1, LANES), 1)
    lane_i = lane & (IDX_DIM - 1)

    @pl.when(qb == 0)
    def _prep():
        rows = min(256, seq)

        def body(r, carry):
            sl = pl.ds(pl.multiple_of(r * rows, rows), rows)
            k = sm_ref[sl, S_K:S_K + LANES]
            k_ref[sl, :] = (k * ca_ref[sl, :] + _swap_attn(k, lane) * sa_ref[sl, :]).astype(BF16)
            v_ref[sl, :] = sm_ref[sl, S_V:S_V + LANES].astype(BF16)
            ki = sm_ref[sl, S_KI:S_KI + LANES]
            kir = ki * ci_ref[sl, :] + _swap_idx(ki, lane_i) * si_ref[sl, :]
            kil = jnp.where(lane < IDX_DIM, kir, 0.0)
            kil_ref[sl, :] = kil.astype(BF16)
            kih_ref[sl, :] = pltpu.roll(kil, IDX_DIM, 1).astype(BF16)
            return carry

        lax.fori_loop(0, seq // rows, body, 0)

    row0 = pl.multiple_of(qb * tq, tq)
    qrows = pl.ds(row0, tq)
    ca_q, sa_q = ca_ref[qrows, :], sa_ref[qrows, :]
    ci_q, si_q = ci_ref[qrows, :], si_ref[qrows, :]

    n_chunks = (row0 + tq + kc - 1) // kc

    def chunk_rows(c):
        return pl.ds(pl.multiple_of(c * kc, kc), kc)

    w_t = sm_ref[qrows, S_WI:S_WI + LANES].T * (IDX_DIM ** -0.5 * IDX_HEADS ** -0.5)
    for pr in range(IDX_HEADS // 2):
        x = qi_ref[:, pr * LANES:(pr + 1) * LANES].astype(F32)
        xr_ref[pr] = (x * ci_q + _swap_idx(x, lane_i) * si_q).astype(BF16)
    qpos = row0 + lax.broadcasted_iota(jnp.int32, (kc, tq), 1)
    kpos0 = lax.broadcasted_iota(jnp.int32, (kc, tq), 0)

    def index_chunk(c, carry):
        ks = chunk_rows(c)
        kil, kih = kil_ref[ks, :], kih_ref[ks, :]
        score = jnp.zeros((kc, tq), F32)
        for pr in range(IDX_HEADS // 2):
            xr = xr_ref[pr]
            score = score + (jnp.maximum(_dot_nt(kil, xr), 0.0) * w_t[2 * pr:2 * pr + 1, :]
                             + jnp.maximum(_dot_nt(kih, xr), 0.0) * w_t[2 * pr + 1:2 * pr + 2, :])
        bits = lax.bitcast_convert_type(score, jnp.int32)
        skey = jnp.where(bits >= 0, bits, bits ^ jnp.int32(0x7FFFFFFF))
        key_ref[ks, :] = jnp.where(kpos0 + c * kc <= qpos, skey, jnp.int32(INT_MIN))
        return carry

    lax.fori_loop(0, n_chunks, index_chunk, 0)

    def search(i, thr):
        cand = thr ^ lax.shift_left(jnp.int32(1), 31 - i)

        def count_chunk(c, acc):
            hit = jnp.where(key_ref[chunk_rows(c), :] >= cand, 1.0, 0.0)
            return acc + _tree_sum([hit[j * 8:(j + 1) * 8, :] for j in range(kc // 8)])

        acc = lax.fori_loop(0, n_chunks, count_chunk, jnp.zeros((8, tq), F32))
        return jnp.where(jnp.sum(acc, axis=0, keepdims=True) >= float(topk), cand, thr)

    thr = lax.fori_loop(0, 32, search, jnp.full((1, tq), INT_MIN, jnp.int32))

    def bias_chunk(c, carry):
        keep = key_ref[chunk_rows(c), :] >= thr
        bias = jnp.where(keep, jnp.where(kpos0 + c * kc <= qpos, 0.0, -jnp.inf), -jnp.inf)
        for kb in range(kc // LANES):
            bias_ref[c, :, kb * LANES:(kb + 1) * LANES] = bias[kb * LANES:(kb + 1) * LANES, :].T
        return carry

    lax.fori_loop(0, n_chunks, bias_chunk, 0)

    for h in range(DSA_HEADS):
        x = q_ref[:, h * LANES:(h + 1) * LANES].astype(F32)
        qs_ref[h * tq:(h + 1) * tq, :] = (x * ca_q + _swap_attn(x, lane) * sa_q).astype(BF16)

    scale2 = DSA_HEAD_DIM ** -0.5 * math.log2(math.e)
    m_ref[...] = jnp.full_like(m_ref, -jnp.inf)
    s_ref[...] = jnp.zeros_like(s_ref)
    acc_ref[...] = jnp.zeros_like(acc_ref)

    def logits_chunk(c, carry):
        lg = _dot_nt(qs_ref[...], k_ref[chunk_rows(c), :]) * scale2
        lg = lg + jnp.concatenate([bias_ref[c]] * DSA_HEADS, axis=0)
        l_ref[c] = lg
        part = lg[:, 0:LANES]
        for kb in range(1, kc // LANES):
            part = jnp.maximum(part, lg[:, kb * LANES:(kb + 1) * LANES])
        m_ref[...] = jnp.maximum(m_ref[...], part)
        return carry

    lax.fori_loop(0, n_chunks, logits_chunk, 0)
    m_ref[...] = jnp.broadcast_to(jnp.max(m_ref[...], axis=-1, keepdims=True), m_ref.shape)

    def value_chunk(c, carry):
        e = jnp.exp2(l_ref[c] - jnp.concatenate([m_ref[...]] * (kc // LANES), axis=1))
        s_ref[...] += _tree_sum([e[:, kb * LANES:(kb + 1) * LANES] for kb in range(kc // LANES)])
        acc_ref[...] += _dot(e.astype(BF16), v_ref[chunk_rows(c), :])
        return carry

    lax.fori_loop(0, n_chunks, value_chunk, 0)
    for h in range(DSA_HEADS):
        hs = slice(h * tq, (h + 1) * tq)
        inv = 1.0 / jnp.sum(s_ref[hs, :], axis=-1, keepdims=True)
        o_ref[:, h * LANES:(h + 1) * LANES] = (acc_ref[hs, :] * inv).astype(o_ref.dtype)


def _dsa(proj, narrow, tables, *, bsz, seq):
    tq = DSA_QBLOCK
    kc = min(256, seq)
    topk = min(DSA_TOPK_MAX, seq // 4)
    nq = seq // tq
    tab_spec = pl.BlockSpec((seq, LANES), lambda b, i: (b, 0))
    return pl.pallas_call(
        functools.partial(_dsa_kernel, topk=topk, tq=tq, kc=kc),
        grid=(bsz, nq),
        in_specs=[pl.BlockSpec((tq, DSA_WIDTH), lambda b, i: (b * nq + i, P_Q // DSA_WIDTH)),
                  pl.BlockSpec((tq, IDX_HEADS * IDX_DIM), lambda b, i: (b * nq + i, P_QI // (IDX_HEADS * IDX_DIM))),
                  pl.BlockSpec((seq, S_WIDTH), lambda b, i: (b, 0)),
                  tab_spec, tab_spec, tab_spec, tab_spec],
        out_specs=pl.BlockSpec((tq, DSA_WIDTH), lambda b, i: (b * nq + i, 0)),
        out_shape=jax.ShapeDtypeStruct((bsz * seq, DSA_WIDTH), BF16),
        scratch_shapes=[pltpu.VMEM((seq, LANES), BF16),
                        pltpu.VMEM((seq, LANES), BF16),
                        pltpu.VMEM((seq, LANES), BF16),
                        pltpu.VMEM((seq, LANES), BF16),
                        pltpu.VMEM((IDX_HEADS // 2, tq, LANES), BF16),
                        pltpu.VMEM((seq, tq), jnp.int32),
                        pltpu.VMEM((seq // kc, tq, kc), F32),
                        pltpu.VMEM((DSA_HEADS * tq, LANES), BF16),
                        pltpu.VMEM((seq // kc, DSA_HEADS * tq, kc), F32),
                        pltpu.VMEM((DSA_HEADS * tq, LANES), F32),
                        pltpu.VMEM((DSA_HEADS * tq, LANES), F32),
                        pltpu.VMEM((DSA_HEADS * tq, LANES), F32)],
        compiler_params=_cparams("parallel", "arbitrary"),
        name="dsa",
    )(proj, proj, narrow, *tables)


def _split_bf16(x, parts):
    out = []
    for _ in range(parts - 1):
        hi = x.astype(BF16)
        out.append(hi)
        x = x - hi.astype(F32)
    out.append(x.astype(BF16))
    return out


def _expand_heads(x, e_ref, parts):
    pieces = [p.astype(F32) for p in _split_bf16(x, parts)]
    k = parts * SSD_HEADS
    k_pad = -k % LANES
    if k_pad:
        pieces.append(jnp.zeros((x.shape[0], k_pad), F32))
    lhs = jnp.concatenate(pieces, axis=1).astype(BF16)
    return _dot(lhs, e_ref[0:k + k_pad, :])


def _ssd_kernel(z_ref, xs_ref, bc_ref, sm_ref, cwx_ref, cwb_ref, cbx_ref, cbb_ref,
                dtb_ref, alog_ref, dskip_ref, ng_ref, e_ref, shift_ref, o_ref,
                xcat_ref, bcat_ref, state_ref, y_ref):
    c = pl.program_id(1)
    q = SSD_CHUNK
    tail = SSD_TAIL

    @pl.when(c == 0)
    def _():
        xcat_ref[q:, :] = jnp.zeros((xcat_ref.shape[0] - q, SSD_INNER), xcat_ref.dtype)
        bcat_ref[q:, :] = jnp.zeros((bcat_ref.shape[0] - q, SSD_BC), bcat_ref.dtype)
        state_ref[...] = jnp.zeros_like(state_ref)

    xcat_ref[0:q, :] = xs_ref[...]
    bcat_ref[0:q, :] = bc_ref[...]

    def conv_silu(cat_ref, w_ref, b_ref):
        width = cat_ref.shape[1]
        delayed = _dot(shift_ref[...], cat_ref[...]).reshape(q // 8, SSD_CONV, 8, width)
        acc = b_ref[...] + w_ref[SSD_CONV - 1:SSD_CONV, :] * delayed[:, 0]
        for j in range(1, SSD_CONV):
            acc = acc + w_ref[SSD_CONV - 1 - j:SSD_CONV - j, :] * delayed[:, j]
        acc = acc.reshape(q, width)
        return acc * _sigmoid(acc)

    xs = conv_silu(xcat_ref, cwx_ref, cbx_ref)
    bc = conv_silu(bcat_ref, cwb_ref, cbb_ref)
    xcat_ref[q:q + tail, :] = xcat_ref[q - tail:q, :]
    bcat_ref[q:q + tail, :] = bcat_ref[q - tail:q, :]

    pre = sm_ref[:, S_DT:S_DT + SSD_HEADS] + dtb_ref[...]
    dt = jnp.maximum(pre, 0.0) + jnp.log1p(jnp.exp(-jnp.abs(pre)))
    da = dt * (-jnp.exp(alog_ref[...]))
    li = lax.broadcasted_iota(jnp.int32, (q, q), 0)
    si = lax.broadcasted_iota(jnp.int32, (q, q), 1)
    causal = li >= si
    tri = jnp.where(causal, 1.0, 0.0).astype(BF16)
    a_cs = sum(_dot(tri, part) for part in _split_bf16(da, 3))
    a_cs_t = jnp.concatenate([a_cs, jnp.zeros_like(a_cs)], axis=1).T

    dt_x = _expand_heads(dt, e_ref, 2)
    acs_x = _expand_heads(a_cs, e_ref, 3)
    ea_x = jnp.exp(acs_x)
    dte_x = jnp.exp(acs_x[q - 1:q, :] - acs_x)
    xdt = xs * dt_x
    xdt_b = xdt.astype(BF16)
    xw_b = (xdt * dte_x).astype(BF16)
    cd_x = ea_x[q - 1:q, :]

    gw = SSD_GROUP_WIDTH
    for g in range(SSD_GROUPS):
        bm = bc[:, g * SSD_STATE:(g + 1) * SSD_STATE]
        cm_b = bc[:, (SSD_GROUPS + g) * SSD_STATE:(SSD_GROUPS + g + 1) * SSD_STATE].astype(BF16)
        cb = _dot_nt(cm_b, bm.astype(BF16))
        st = state_ref[g]
        y_ref[:, g * gw:(g + 1) * gw] = _dot(cm_b, st.astype(BF16)) * ea_x[:, g * gw:(g + 1) * gw]
        for r in range(SSD_HEADS_PER_GROUP):
            h = g * SSD_HEADS_PER_GROUP + r
            seg = a_cs[:, h:h + 1] - a_cs_t[h:h + 1, :]
            decay = jnp.exp(jnp.where(causal, seg, -jnp.inf))
            lo, hi = h * SSD_HEAD_DIM, (h + 1) * SSD_HEAD_DIM
            y_ref[:, lo:hi] += _dot((cb * decay).astype(BF16), xdt_b[:, lo:hi])
        upd = _dot(bm.T.astype(BF16), xw_b[:, g * gw:(g + 1) * gw])
        state_ref[g] = st * cd_x[:, g * gw:(g + 1) * gw] + upd

    z = z_ref[...].astype(F32)
    y = (y_ref[...] + xs * dskip_ref[...]) * (z * _sigmoid(z))
    ms = jnp.mean(y * y, axis=-1, keepdims=True)
    o_ref[...] = (y * lax.rsqrt(ms + NORM_EPS) * ng_ref[...]).astype(o_ref.dtype)


def _ssd(proj, narrow, conv_w, conv_b, dt_bias, a_log, d_skip, norm_g, *, bsz, seq):
    q = SSD_CHUNK
    nc = seq // q
    one_hot = np.repeat(np.eye(SSD_HEADS, dtype=np.float32), SSD_HEAD_DIM, axis=1)
    expand = jnp.asarray(np.concatenate([one_hot] * 3 + [np.zeros_like(one_hot)], axis=0), BF16)
    shift = np.zeros((SSD_CONV * q, SSD_STAGE_ROWS), np.float32)
    t = np.arange(q)
    for j in range(SSD_CONV):
        shift[(t // 8) * (8 * SSD_CONV) + j * 8 + t % 8, np.where(t >= j, t - j, q + SSD_TAIL + t - j)] = 1.0
    shift = jnp.asarray(shift, BF16)
    row = lambda b, c: (b * nc + c)
    const = lambda b, c: (0, 0)
    return pl.pallas_call(
        _ssd_kernel,
        grid=(bsz, nc),
        in_specs=[pl.BlockSpec((q, SSD_INNER), lambda b, c: (row(b, c), P_Z // SSD_INNER)),
                  pl.BlockSpec((q, SSD_INNER), lambda b, c: (row(b, c), P_XS // SSD_INNER)),
                  pl.BlockSpec((q, SSD_BC), lambda b, c: (row(b, c), P_BC // SSD_BC)),
                  pl.BlockSpec((q, S_WIDTH), lambda b, c: (row(b, c), 0)),
                  pl.BlockSpec((SSD_CONV, SSD_INNER), const),
                  pl.BlockSpec((SSD_CONV, SSD_BC), const),
                  pl.BlockSpec((1, SSD_INNER), const),
                  pl.BlockSpec((1, SSD_BC), const),
                  pl.BlockSpec((1, SSD_HEADS), const),
                  pl.BlockSpec((1, SSD_HEADS), const),
                  pl.BlockSpec((1, SSD_INNER), const),
                  pl.BlockSpec((1, SSD_INNER), const),
                  pl.BlockSpec((4 * SSD_HEADS, SSD_INNER), const),
                  pl.BlockSpec((SSD_CONV * q, SSD_STAGE_ROWS), const)],
        out_specs=pl.BlockSpec((q, SSD_INNER), lambda b, c: (row(b, c), 0)),
        out_shape=jax.ShapeDtypeStruct((bsz * seq, SSD_INNER), BF16),
        scratch_shapes=[pltpu.VMEM((SSD_STAGE_ROWS, SSD_INNER), BF16),
                        pltpu.VMEM((SSD_STAGE_ROWS, SSD_BC), BF16),
                        pltpu.VMEM((SSD_GROUPS, SSD_STATE, SSD_GROUP_WIDTH), F32),
                        pltpu.VMEM((q, SSD_INNER), F32)],
        compiler_params=_cparams("parallel", "arbitrary"),
        name="ssd",
    )(proj, proj, proj, narrow,
      conv_w[:, :SSD_INNER], conv_w[:, SSD_INNER:], conv_b[None, :SSD_INNER], conv_b[None, SSD_INNER:],
      dt_bias[None, :], a_log[None, :], jnp.repeat(d_skip, SSD_HEAD_DIM)[None, :], norm_g[None, :], expand, shift)


def _regroup_kernel(w_ref, wide_ref, narrow_ref, *, step):
    def move(dst_ref, src, dst, width):
        for c in range(0, width, step):
            n = min(step, width - c)
            dst_ref[dst + c:dst + c + n, :] = w_ref[src + c:src + c + n, :].astype(dst_ref.dtype)

    for src, dst, width in _WIDE_PIECES:
        move(wide_ref, src, dst, width)
    used = S_WI + IDX_HEADS
    narrow_ref[used:, :] = jnp.zeros((S_WIDTH - used, narrow_ref.shape[1]), narrow_ref.dtype)
    for src, dst, width in _NARROW_PIECES:
        move(narrow_ref, src, dst, width)


def _regroup_w_in(w_in_t, *, cols=128):
    depth, n, d = w_in_t.shape
    return pl.pallas_call(
        functools.partial(_regroup_kernel, step=1024),
        grid=(depth, d // cols),
        in_specs=[pl.BlockSpec((None, n, cols), lambda l, c: (l, 0, c))],
        out_specs=[pl.BlockSpec((None, P_WIDTH, cols), lambda l, c: (l, 0, c)),
                   pl.BlockSpec((None, S_WIDTH, cols), lambda l, c: (l, 0, c))],
        out_shape=[jax.ShapeDtypeStruct((depth, P_WIDTH, d), BF16),
                   jax.ShapeDtypeStruct((depth, S_WIDTH, d), BF16)],
        compiler_params=_cparams("parallel", "parallel"),
        name="regroup_w_in",
    )(w_in_t)


def kernel(x, mem, positions, ffn1_norm, w_ffn1_in, w_ffn1_out, mix_norm, w_in, conv_w, conv_b, dt_bias, a_log,
           d_skip, ssd_norm, mem_norm, w_mem_kv, w_br_ssd, w_br_dsa, w_br_mem, w_out, ffn2_norm, w_ffn2_in,
           w_ffn2_out, final_norm):
    bsz, seq, d = x.shape
    depth = w_in.shape[0]
    assert d == D_MODEL and seq % DSA_QBLOCK == 0 and seq % SSD_CHUNK == 0 and mem.shape[1] == MEM_LEN

    w_p, w_n = _regroup_w_in(jnp.swapaxes(w_in, 1, 2))
    w1i, w1o = w_ffn1_in.astype(BF16), w_ffn1_out.astype(BF16)
    w2i, w2o = w_ffn2_in.astype(BF16), w_ffn2_out.astype(BF16)
    wkv = w_mem_kv.astype(BF16)
    wbs, wbd, wbm, wo = (w.astype(BF16) for w in (w_br_ssd, w_br_dsa, w_br_mem, w_out))

    tables = _rope_tables(positions)
    xf = x.reshape(bsz * seq, d)
    memf = mem.reshape(bsz * MEM_LEN, d)
    for l in range(depth):
        xf = _ffn(xf, ffn1_norm[l], w1i, w1o, final_norm, l, final_norm=False)
        proj = _norm_matmul(xf, mix_norm[l], w_p, l, tm=1024, tn=1024, out_dtype=BF16, name="in_proj",
                            w_is_transposed=True)
        narrow = _norm_matmul(xf, mix_norm[l], w_n, l, tm=1024, tn=S_WIDTH, out_dtype=F32, name="in_proj_narrow",
                              w_is_transposed=True)
        y_ssd = _ssd(proj, narrow, conv_w[l], conv_b[l], dt_bias[l], a_log[l], d_skip[l], ssd_norm[l],
                     bsz=bsz, seq=seq)
        y_dsa = _dsa(proj, narrow, tables, bsz=bsz, seq=seq)
        kv = _norm_matmul(memf, mem_norm[l], wkv, l, tm=1024, tn=512, out_dtype=BF16, name="mem_kv")
        y_mem = _memattn(proj, kv, bsz=bsz, seq=seq)
        merged = _branch_merge(y_ssd, y_dsa, y_mem, proj, wbs, wbd, wbm, l)
        xf = _out_proj(xf, merged, wo, l)
        xf = _ffn(xf, ffn2_norm[l], w2i, w2o, final_norm, l, final_norm=(l == depth - 1))
    return xf.reshape(bsz, seq, d)
```

```python
import functools
import math

import numpy as np
import jax
import jax.numpy as jnp
from jax import lax
from jax.experimental import pallas as pl
from jax.experimental.pallas import tpu as pltpu

F32 = jnp.float32
BF16 = jnp.bfloat16

D_MODEL = 2048
MEM_LEN = 256
NORM_EPS = 1e-6
ROPE_THETA = 500000.0
D_FF = 5632

SSD_INNER = 4096
SSD_HEAD_DIM = 64
SSD_HEADS = 64
SSD_GROUPS = 8
SSD_HEADS_PER_GROUP = 8
SSD_STATE = 128
SSD_CONV = 4
SSD_CHUNK = 128
SSD_GROUP_WIDTH = SSD_HEADS_PER_GROUP * SSD_HEAD_DIM
SSD_BC = 2 * SSD_GROUPS * SSD_STATE
SSD_TAIL = 16
SSD_STAGE_ROWS = 256

DSA_HEADS = 16
DSA_HEAD_DIM = 128
DSA_WIDTH = 2048
DSA_ROPE = 32
IDX_HEADS = 16
IDX_DIM = 64
IDX_ROPE = 16
DSA_TOPK_MAX = 256
DSA_QBLOCK = 128

XA_HEADS = 4
XA_HEAD_DIM = 512
XA_WIDTH = 2048

_IN_SPLITS = (SSD_INNER, SSD_INNER + SSD_BC, SSD_HEADS, DSA_WIDTH, DSA_HEAD_DIM, DSA_HEAD_DIM,
              IDX_HEADS * IDX_DIM, IDX_DIM, IDX_HEADS, XA_WIDTH, 3 * D_MODEL)
_IN_OFF = np.concatenate([[0], np.cumsum(_IN_SPLITS)]).tolist()

P_Z = 0
P_XS = 4096
P_Q = 8192
P_QMEM = 10240
P_GATE = 12288
P_BC = 18432
P_QI = 20480
P_WIDTH = 21504
S_K, S_V, S_KI, S_DT, S_WI = 0, 128, 256, 320, 384
S_WIDTH = 512
_O = _IN_OFF
_WIDE_PIECES = ((_O[0], P_Z, SSD_INNER), (_O[1], P_XS, SSD_INNER), (_O[3], P_Q, DSA_WIDTH),
                (_O[9], P_QMEM, XA_WIDTH), (_O[10], P_GATE, 3 * D_MODEL), (_O[1] + SSD_INNER, P_BC, SSD_BC),
                (_O[6], P_QI, IDX_HEADS * IDX_DIM))
_NARROW_PIECES = ((_O[4], S_K, DSA_HEAD_DIM), (_O[5], S_V, DSA_HEAD_DIM), (_O[7], S_KI, IDX_DIM),
                  (_O[2], S_DT, SSD_HEADS), (_O[8], S_WI, IDX_HEADS))

LANES = 128
V7X_VMEM_LIMIT = 56 * 1024 * 1024
INT_MIN = -(2 ** 31)


def _cparams(*sem):
    return pltpu.CompilerParams(dimension_semantics=sem, vmem_limit_bytes=V7X_VMEM_LIMIT)


def _dot(a, b):
    return jnp.dot(a, b, preferred_element_type=F32)


def _dot_nt(a, b):
    return lax.dot_general(a, b, (((1,), (1,)), ((), ())), preferred_element_type=F32)


def _sigmoid(x):
    return 1.0 / (1.0 + jnp.exp(-x))


def _rms_rows_to(x_ref, g_ref, h_ref, rows):
    n = x_ref.shape[0] // rows

    def body(r, carry):
        sl = pl.ds(pl.multiple_of(r * rows, rows), rows)
        x = x_ref[sl, :]
        ms = jnp.mean(x * x, axis=-1, keepdims=True)
        h_ref[sl, :] = (x * lax.rsqrt(ms + NORM_EPS) * g_ref[...]).astype(h_ref.dtype)
        return carry

    lax.fori_loop(0, n, body, 0)


def _norm_matmul_kernel(x_ref, g_ref, w_ref, o_ref, h_ref, *, w_is_transposed):
    @pl.when(pl.program_id(1) == 0)
    def _():
        _rms_rows_to(x_ref, g_ref, h_ref, min(128, x_ref.shape[0]))

    mm = _dot_nt if w_is_transposed else _dot
    o_ref[...] = mm(h_ref[...], w_ref[...]).astype(o_ref.dtype)


def _norm_matmul(x, g, w, layer, *, tm, tn, out_dtype, name, w_is_transposed=False):
    m, d = x.shape
    n = w.shape[1] if w_is_transposed else w.shape[2]
    tm = min(tm, m)
    if w_is_transposed:
        w_spec = pl.BlockSpec((None, tn, d), lambda i, j: (layer, j, 0))
    else:
        w_spec = pl.BlockSpec((None, d, tn), lambda i, j: (layer, 0, j))
    return pl.pallas_call(
        functools.partial(_norm_matmul_kernel, w_is_transposed=w_is_transposed),
        grid=(m // tm, n // tn),
        in_specs=[pl.BlockSpec((tm, d), lambda i, j: (i, 0)),
                  pl.BlockSpec((1, d), lambda i, j: (0, 0)),
                  w_spec],
        out_specs=pl.BlockSpec((tm, tn), lambda i, j: (i, j)),
        out_shape=jax.ShapeDtypeStruct((m, n), out_dtype),
        scratch_shapes=[pltpu.VMEM((tm, d), BF16)],
        compiler_params=_cparams("parallel", "arbitrary"),
        name=name,
    )(x, g.reshape(1, d), w)


def _ffn_kernel(x_ref, g_ref, wg_ref, wu_ref, wo_ref, fg_ref, o_ref, h_ref, *, final_norm):
    j = pl.program_id(1)

    @pl.when(j == 0)
    def _():
        _rms_rows_to(x_ref, g_ref, h_ref, min(128, x_ref.shape[0]))
        o_ref[...] = x_ref[...]

    h = h_ref[...]
    gate = _dot(h, wg_ref[...])
    up = _dot(h, wu_ref[...])
    act = (gate * _sigmoid(gate) * up).astype(BF16)
    o_ref[...] += 0.5 * _dot(act, wo_ref[...])

    if final_norm:
        @pl.when(j == pl.num_programs(1) - 1)
        def _():
            _rms_rows_to(o_ref, fg_ref, o_ref, min(128, o_ref.shape[0]))


def _ffn(x, g, w_in, w_out, final_g, layer, *, final_norm, tm=1024, tf=512):
    m, d = x.shape
    f = w_out.shape[1]
    tm = min(tm, m)
    nf = f // tf
    return pl.pallas_call(
        functools.partial(_ffn_kernel, final_norm=final_norm),
        grid=(m // tm, nf),
        in_specs=[pl.BlockSpec((tm, d), lambda i, j: (i, 0)),
                  pl.BlockSpec((1, d), lambda i, j: (0, 0)),
                  pl.BlockSpec((None, d, tf), lambda i, j: (layer, 0, j)),
                  pl.BlockSpec((None, d, tf), lambda i, j: (layer, 0, j + nf)),
                  pl.BlockSpec((None, tf, d), lambda i, j: (layer, j, 0)),
                  pl.BlockSpec((1, d), lambda i, j: (0, 0))],
        out_specs=pl.BlockSpec((tm, d), lambda i, j: (i, 0)),
        out_shape=jax.ShapeDtypeStruct((m, d), F32),
        scratch_shapes=[pltpu.VMEM((tm, d), BF16)],
        compiler_params=_cparams("parallel", "arbitrary"),
        name="ffn_final" if final_norm else "ffn",
    )(x, g.reshape(1, d), w_in, w_in, w_out, final_g.reshape(1, d))


def _branch_merge_kernel(ys_ref, yd_ref, ym_ref, g0_ref, g1_ref, g2_ref, ws_ref, wd_ref, wm_ref, o_ref):
    merged = (_sigmoid(g0_ref[...].astype(F32)) * _dot(ys_ref[...], ws_ref[...])
              + _sigmoid(g1_ref[...].astype(F32)) * _dot(yd_ref[...], wd_ref[...])
              + _sigmoid(g2_ref[...].astype(F32)) * _dot(ym_ref[...], wm_ref[...]))
    o_ref[...] = merged.astype(o_ref.dtype)


def _branch_merge(y_ssd, y_dsa, y_mem, proj, w_s, w_d, w_m, layer, *, tm=1024, tn=256):
    m = y_ssd.shape[0]
    d = D_MODEL
    tm = min(tm, m)
    gate_blk = P_GATE // tn
    per_branch = d // tn
    return pl.pallas_call(
        _branch_merge_kernel,
        grid=(m // tm, d // tn),
        in_specs=[pl.BlockSpec((tm, SSD_INNER), lambda i, j: (i, 0)),
                  pl.BlockSpec((tm, DSA_WIDTH), lambda i, j: (i, 0)),
                  pl.BlockSpec((tm, XA_WIDTH), lambda i, j: (i, 0)),
                  pl.BlockSpec((tm, tn), lambda i, j: (i, gate_blk + j)),
                  pl.BlockSpec((tm, tn), lambda i, j: (i, gate_blk + per_branch + j)),
                  pl.BlockSpec((tm, tn), lambda i, j: (i, gate_blk + 2 * per_branch + j)),
                  pl.BlockSpec((None, SSD_INNER, tn), lambda i, j: (layer, 0, j)),
                  pl.BlockSpec((None, DSA_WIDTH, tn), lambda i, j: (layer, 0, j)),
                  pl.BlockSpec((None, XA_WIDTH, tn), lambda i, j: (layer, 0, j))],
        out_specs=pl.BlockSpec((tm, tn), lambda i, j: (i, j)),
        out_shape=jax.ShapeDtypeStruct((m, d), BF16),
        compiler_params=_cparams("parallel", "parallel"),
        name="branch_merge",
    )(y_ssd, y_dsa, y_mem, proj, proj, proj, w_s, w_d, w_m)


def _out_proj_kernel(x_ref, m_ref, w_ref, o_ref):
    o_ref[...] = x_ref[...] + _dot(m_ref[...], w_ref[...])


def _out_proj(x, merged, w_o, layer, *, tm=512):
    m, d = x.shape
    tm = min(tm, m)
    return pl.pallas_call(
        _out_proj_kernel,
        grid=(m // tm,),
        in_specs=[pl.BlockSpec((tm, d), lambda i: (i, 0)),
                  pl.BlockSpec((tm, d), lambda i: (i, 0)),
                  pl.BlockSpec((None, d, d), lambda i: (layer, 0, 0))],
        out_specs=pl.BlockSpec((tm, d), lambda i: (i, 0)),
        out_shape=jax.ShapeDtypeStruct((m, d), F32),
        compiler_params=_cparams("parallel"),
        name="out_proj",
    )(x, merged, w_o)


def _memattn_kernel(q_ref, k_ref, v_ref, o_ref, *, rows):
    k = k_ref[...]
    v = v_ref[...]
    scale = XA_HEAD_DIM ** -0.5

    def body(r, carry):
        sl = pl.ds(pl.multiple_of(r * rows, rows), rows)
        q = q_ref[sl, :]
        logits = _dot_nt(q, k) * scale
        mx = jnp.max(logits, axis=-1, keepdims=True)
        e = jnp.exp(logits - mx)
        p = e * (1.0 / jnp.sum(e, axis=-1, keepdims=True))
        o_ref[sl, :] = _dot(p.astype(BF16), v).astype(o_ref.dtype)
        return carry

    lax.fori_loop(0, q_ref.shape[0] // rows, body, 0)


def _memattn(proj, kv, *, bsz, seq):
    qblk = P_QMEM // XA_HEAD_DIM
    return pl.pallas_call(
        functools.partial(_memattn_kernel, rows=min(256, seq)),
        grid=(bsz, XA_HEADS),
        in_specs=[pl.BlockSpec((seq, XA_HEAD_DIM), lambda b, h: (b, qblk + h)),
                  pl.BlockSpec((MEM_LEN, XA_HEAD_DIM), lambda b, h: (b, h)),
                  pl.BlockSpec((MEM_LEN, XA_HEAD_DIM), lambda b, h: (b, XA_HEADS + h))],
        out_specs=pl.BlockSpec((seq, XA_HEAD_DIM), lambda b, h: (b, h)),
        out_shape=jax.ShapeDtypeStruct((bsz * seq, XA_WIDTH), BF16),
        compiler_params=_cparams("parallel", "parallel"),
        name="memattn",
    )(proj, kv, kv)


def _rope_tables_kernel(pos_ref, ca_ref, sa_ref, ci_ref, si_ref, *, rows):
    lane = lax.broadcasted_iota(jnp.int32, (1, LANES), 1)
    ln_theta = math.log(ROPE_THETA)
    inv_a = jnp.exp((lane & (DSA_ROPE // 2 - 1)).astype(F32) * (-2.0 * ln_theta / DSA_ROPE))
    inv_i = jnp.exp((lane & (IDX_ROPE // 2 - 1)).astype(F32) * (-2.0 * ln_theta / IDX_ROPE))
    lane_i = lane & (IDX_DIM - 1)

    def body(r, carry):
        sl = pl.ds(pl.multiple_of(r * rows, rows), rows)
        pos = pos_ref[sl, :].astype(F32)
        ang = pos * inv_a
        c, s = jnp.cos(ang), jnp.sin(ang)
        ca_ref[sl, :] = jnp.where(lane < DSA_ROPE, c, 1.0)
        sa_ref[sl, :] = jnp.where(lane < DSA_ROPE // 2, -s, jnp.where(lane < DSA_ROPE, s, 0.0))
        ang = pos * inv_i
        c, s = jnp.cos(ang), jnp.sin(ang)
        ci_ref[sl, :] = jnp.where(lane_i < IDX_ROPE, c, 1.0)
        si_ref[sl, :] = jnp.where(lane_i < IDX_ROPE // 2, -s, jnp.where(lane_i < IDX_ROPE, s, 0.0))
        return carry

    lax.fori_loop(0, pos_ref.shape[0] // rows, body, 0)


def _rope_tables(positions):
    bsz, seq = positions.shape
    tab = jax.ShapeDtypeStruct((bsz * seq, LANES), F32)
    spec = pl.BlockSpec((seq, LANES), lambda b: (b, 0))
    return pl.pallas_call(
        functools.partial(_rope_tables_kernel, rows=64),
        grid=(bsz,),
        in_specs=[pl.BlockSpec((seq, 1), lambda b: (b, 0))],
        out_specs=[spec, spec, spec, spec],
        out_shape=[tab, tab, tab, tab],
        compiler_params=_cparams("parallel"),
        name="rope_tables",
    )(positions.reshape(bsz * seq, 1))


def _swap_attn(x, lane):
    h = DSA_ROPE // 2
    return jnp.where(lane < h, pltpu.roll(x, LANES - h, 1), pltpu.roll(x, h, 1))


def _swap_idx(x, lane_i):
    h = IDX_ROPE // 2
    return jnp.where(lane_i < h, pltpu.roll(x, LANES - h, 1), pltpu.roll(x, h, 1))


def _tree_sum(parts):
    while len(parts) > 1:
        parts = [parts[a] + parts[a + 1] for a in range(0, len(parts) - 1, 2)] + ([parts[-1]] if len(parts) % 2 else [])
    return parts[0]


def _dsa_kernel(q_ref, qi_ref, sm_ref, ca_ref, sa_ref, ci_ref, si_ref, rota_ref, roti_ref, o_ref,
                k_ref, kil_ref, kih_ref, v_ref, xr_ref, key_ref, lim_ref, bias_ref, qs_ref, l_ref,
                m_ref, s_ref, acc_ref, *, topk, tq, kc):
    qb = pl.program_id(1)
    seq = sm_ref.shape[0]
    lane = lax.broadcasted_iota(jnp.int32, (1, LANES), 1)
    lane_i = lane & (IDX_DIM - 1)

    @pl.when(qb == 0)
    def _prep():
        rows = min(256, seq)

        def body(r, carry):
            sl = pl.ds(pl.multiple_of(r * rows, rows), rows)
            k = sm_ref[sl, S_K:S_K + LANES]
            k_ref[sl, :] = (k * ca_ref[sl, :] + _swap_attn(k, lane) * sa_ref[sl, :]).astype(BF16)
            v_ref[sl, :] = sm_ref[sl, S_V:S_V + LANES].astype(BF16)
            ki = sm_ref[sl, S_KI:S_KI + LANES]
            kir = ki * ci_ref[sl, :] + _swap_idx(ki, lane_i) * si_ref[sl, :]
            kil = jnp.where(lane < IDX_DIM, kir, 0.0)
            kil_ref[sl, :] = kil.astype(BF16)
            kih_ref[sl, :] = pltpu.roll(kil, IDX_DIM, 1).astype(BF16)
            return carry

        lax.fori_loop(0, seq // rows, body, 0)

    row0 = pl.multiple_of(qb * tq, tq)
    qrows = pl.ds(row0, tq)
    ca_q, sa_q = ca_ref[qrows, :], sa_ref[qrows, :]
    ci_q, si_q = ci_ref[qrows, :], si_ref[qrows, :]

    n_chunks = (row0 + tq + kc - 1) // kc

    def chunk_rows(c):
        return pl.ds(pl.multiple_of(c * kc, kc), kc)

    w_t = sm_ref[qrows, S_WI:S_WI + LANES].T * (IDX_DIM ** -0.5 * IDX_HEADS ** -0.5)

    def rotate(x, rot_ref, cos, sin):
        both = _dot(x, rot_ref[...])
        return (both[:, 0:LANES] * cos + both[:, LANES:2 * LANES] * sin).astype(BF16)

    for pr in range(IDX_HEADS // 2):
        xr_ref[pr] = rotate(qi_ref[:, pr * LANES:(pr + 1) * LANES], roti_ref, ci_q, si_q)
    qpos = row0 + lax.broadcasted_iota(jnp.int32, (kc, tq), 1)
    kpos0 = lax.broadcasted_iota(jnp.int32, (kc, tq), 0)

    def index_chunk(c, carry):
        ks = chunk_rows(c)
        kil, kih = kil_ref[ks, :], kih_ref[ks, :]
        score = jnp.zeros((kc, tq), F32)
        for pr in range(IDX_HEADS // 2):
            xr = xr_ref[pr]
            score = score + (jnp.maximum(_dot_nt(kil, xr), 0.0) * w_t[2 * pr:2 * pr + 1, :]
                             + jnp.maximum(_dot_nt(kih, xr), 0.0) * w_t[2 * pr + 1:2 * pr + 2, :])
        bits = lax.bitcast_convert_type(score, jnp.int32)
        skey = jnp.where(bits >= 0, bits, bits ^ jnp.int32(0x7FFFFFFF))
        key_ref[ks, :] = jnp.where(kpos0 + c * kc <= qpos, skey, jnp.int32(INT_MIN))
        return carry

    lax.fori_loop(0, n_chunks, index_chunk, 0)

    def count_keys(pred, also=None):
        def count_chunk(c, acc):
            k, pos = key_ref[chunk_rows(c), :], kpos0 + c * kc
            hit = jnp.where(pred(k, pos), 1.0, 0.0)
            if also is not None:
                hit = jnp.where(also(k, pos), hit, 0.0)
            return acc + _tree_sum([hit[j * 8:(j + 1) * 8, :] for j in range(kc // 8)])

        acc = lax.fori_loop(0, n_chunks, count_chunk, jnp.zeros((8, tq), F32))
        return jnp.sum(acc, axis=0, keepdims=True)

    def search(i, thr):
        cand = thr ^ lax.shift_left(jnp.int32(1), 31 - i)
        return jnp.where(count_keys(lambda k, pos: k >= cand) >= float(topk), cand, thr)

    thr = lax.fori_loop(0, 32, search, jnp.full((1, tq), INT_MIN, jnp.int32))

    need = float(topk) - count_keys(lambda k, pos: k > thr)
    n_tied = count_keys(lambda k, pos: k == thr)
    lim_ref[...] = jnp.full(lim_ref.shape, seq, jnp.int32)
    surplus = jnp.max(jnp.where(thr > INT_MIN, n_tied - need, 0.0))

    @pl.when(surplus > 0.0)
    def _():
        nbits = (seq - 1).bit_length()

        def refine(i, lo):
            cand = lo + lax.shift_left(jnp.int32(1), nbits - 1 - i)
            below = count_keys(lambda k, pos: k == thr, lambda k, pos: pos < cand)
            return jnp.where(below < need, cand, lo)

        lo = lax.fori_loop(0, nbits, refine, jnp.zeros((1, tq), jnp.int32))
        lim_ref[...] = jnp.broadcast_to(lo + 1, lim_ref.shape)

    def bias_chunk(c, carry):
        k = key_ref[chunk_rows(c), :]
        pos = kpos0 + c * kc
        visible = jnp.where(pos <= qpos, 0.0, -jnp.inf)
        tied = jnp.where(k == thr, jnp.where(pos < lim_ref[0:1, :], visible, -jnp.inf), -jnp.inf)
        bias = jnp.where(k > thr, visible, tied)
        for kb in range(kc // LANES):
            bias_ref[c, :, kb * LANES:(kb + 1) * LANES] = bias[kb * LANES:(kb + 1) * LANES, :].T
        return carry

    lax.fori_loop(0, n_chunks, bias_chunk, 0)

    for h in range(DSA_HEADS):
        qs_ref[h * tq:(h + 1) * tq, :] = rotate(q_ref[:, h * LANES:(h + 1) * LANES], rota_ref, ca_q, sa_q)

    scale2 = DSA_HEAD_DIM ** -0.5 * math.log2(math.e)

    def logits_chunk(c, first):
        lg = _dot_nt(qs_ref[...], k_ref[chunk_rows(c), :]) * scale2
        lg = lg + jnp.concatenate([bias_ref[c]] * DSA_HEADS, axis=0)
        l_ref[c] = lg
        part = lg[:, 0:LANES]
        for kb in range(1, kc // LANES):
            part = jnp.maximum(part, lg[:, kb * LANES:(kb + 1) * LANES])
        m_ref[...] = part if first else jnp.maximum(m_ref[...], part)

    logits_chunk(0, True)
    lax.fori_loop(1, n_chunks, lambda c, carry: (logits_chunk(c, False), carry)[1], 0)
    m_ref[...] = jnp.broadcast_to(jnp.max(m_ref[...], axis=-1, keepdims=True), m_ref.shape)

    def value_chunk(c, first):
        e = jnp.exp2(l_ref[c] - jnp.concatenate([m_ref[...]] * (kc // LANES), axis=1))
        rowsum = _tree_sum([e[:, kb * LANES:(kb + 1) * LANES] for kb in range(kc // LANES)])
        out = _dot(e.astype(BF16), v_ref[chunk_rows(c), :])
        s_ref[...] = rowsum if first else s_ref[...] + rowsum
        acc_ref[...] = out if first else acc_ref[...] + out

    value_chunk(0, True)
    lax.fori_loop(1, n_chunks, lambda c, carry: (value_chunk(c, False), carry)[1], 0)
    for h in range(DSA_HEADS):
        hs = slice(h * tq, (h + 1) * tq)
        inv = 1.0 / jnp.sum(s_ref[hs, :], axis=-1, keepdims=True)
        o_ref[:, h * LANES:(h + 1) * LANES] = (acc_ref[hs, :] * inv).astype(o_ref.dtype)


def _dsa(proj, narrow, tables, *, bsz, seq):
    tq = DSA_QBLOCK
    kc = min(256, seq)
    topk = min(DSA_TOPK_MAX, seq // 4)
    nq = seq // tq
    tab_spec = pl.BlockSpec((seq, LANES), lambda b, i: (b, 0))
    rot_spec = pl.BlockSpec((LANES, 2 * LANES), lambda b, i: (0, 0))

    def partner_matrix(period, half):
        p = np.zeros((LANES, LANES), np.float32)
        j = np.arange(LANES)
        first, second = (j % period) < half, ((j % period) >= half) & ((j % period) < 2 * half)
        p[j[first] + half, j[first]] = 1.0
        p[j[second] - half, j[second]] = 1.0
        return jnp.asarray(np.concatenate([np.eye(LANES, dtype=np.float32), p], axis=1), BF16)

    rot_a = partner_matrix(LANES, DSA_ROPE // 2)
    rot_i = partner_matrix(IDX_DIM, IDX_ROPE // 2)
    return pl.pallas_call(
        functools.partial(_dsa_kernel, topk=topk, tq=tq, kc=kc),
        grid=(bsz, nq),
        in_specs=[pl.BlockSpec((tq, DSA_WIDTH), lambda b, i: (b * nq + i, P_Q // DSA_WIDTH)),
                  pl.BlockSpec((tq, IDX_HEADS * IDX_DIM), lambda b, i: (b * nq + i, P_QI // (IDX_HEADS * IDX_DIM))),
                  pl.BlockSpec((seq, S_WIDTH), lambda b, i: (b, 0)),
                  tab_spec, tab_spec, tab_spec, tab_spec, rot_spec, rot_spec],
        out_specs=pl.BlockSpec((tq, DSA_WIDTH), lambda b, i: (b * nq + i, 0)),
        out_shape=jax.ShapeDtypeStruct((bsz * seq, DSA_WIDTH), BF16),
        scratch_shapes=[pltpu.VMEM((seq, LANES), BF16),
                        pltpu.VMEM((seq, LANES), BF16),
                        pltpu.VMEM((seq, LANES), BF16),
                        pltpu.VMEM((seq, LANES), BF16),
                        pltpu.VMEM((IDX_HEADS // 2, tq, LANES), BF16),
                        pltpu.VMEM((seq, tq), jnp.int32),
                        pltpu.VMEM((8, tq), jnp.int32),
                        pltpu.VMEM((seq // kc, tq, kc), F32),
                        pltpu.VMEM((DSA_HEADS * tq, LANES), BF16),
                        pltpu.VMEM((seq // kc, DSA_HEADS * tq, kc), F32),
                        pltpu.VMEM((DSA_HEADS * tq, LANES), F32),
                        pltpu.VMEM((DSA_HEADS * tq, LANES), F32),
                        pltpu.VMEM((DSA_HEADS * tq, LANES), F32)],
        compiler_params=_cparams("parallel", "arbitrary"),
        name="dsa",
    )(proj, proj, narrow, *tables, rot_a, rot_i)


def _split_bf16(x, parts):
    out = []
    for _ in range(parts - 1):
        hi = x.astype(BF16)
        out.append(hi)
        x = x - hi.astype(F32)
    out.append(x.astype(BF16))
    return out


def _expand_heads(x, e_ref, parts):
    pieces = [p.astype(F32) for p in _split_bf16(x, parts)]
    k = parts * SSD_HEADS
    k_pad = -k % LANES
    if k_pad:
        pieces.append(jnp.zeros((x.shape[0], k_pad), F32))
    lhs = jnp.concatenate(pieces, axis=1).astype(BF16)
    return _dot(lhs, e_ref[0:k + k_pad, :])


def _ssd_kernel(z_ref, xs_ref, bc_ref, sm_ref, cwx_ref, cwb_ref, cbx_ref, cbb_ref,
                dtb_ref, alog_ref, dskip_ref, ng_ref, e_ref, shift_ref, o_ref,
                xcat_ref, bcat_ref, state_ref, y_ref):
    c = pl.program_id(1)
    q = SSD_CHUNK
    tail = SSD_TAIL

    @pl.when(c == 0)
    def _():
        xcat_ref[q:, :] = jnp.zeros((xcat_ref.shape[0] - q, SSD_INNER), xcat_ref.dtype)
        bcat_ref[q:, :] = jnp.zeros((bcat_ref.shape[0] - q, SSD_BC), bcat_ref.dtype)
        state_ref[...] = jnp.zeros_like(state_ref)

    xcat_ref[0:q, :] = xs_ref[...]
    bcat_ref[0:q, :] = bc_ref[...]

    def conv_silu(cat_ref, w_ref, b_ref):
        width = cat_ref.shape[1]
        delayed = _dot(shift_ref[...], cat_ref[...]).reshape(q // 8, SSD_CONV, 8, width)
        acc = b_ref[...] + w_ref[SSD_CONV - 1:SSD_CONV, :] * delayed[:, 0]
        for j in range(1, SSD_CONV):
            acc = acc + w_ref[SSD_CONV - 1 - j:SSD_CONV - j, :] * delayed[:, j]
        acc = acc.reshape(q, width)
        return acc * _sigmoid(acc)

    xs = conv_silu(xcat_ref, cwx_ref, cbx_ref)
    bc = conv_silu(bcat_ref, cwb_ref, cbb_ref)
    xcat_ref[q:q + tail, :] = xcat_ref[q - tail:q, :]
    bcat_ref[q:q + tail, :] = bcat_ref[q - tail:q, :]

    pre = sm_ref[:, S_DT:S_DT + SSD_HEADS] + dtb_ref[...]
    dt = jnp.maximum(pre, 0.0) + jnp.log1p(jnp.exp(-jnp.abs(pre)))
    da = dt * (-jnp.exp(alog_ref[...]))
    li = lax.broadcasted_iota(jnp.int32, (q, q), 0)
    si = lax.broadcasted_iota(jnp.int32, (q, q), 1)
    causal = li >= si
    tri = jnp.where(causal, 1.0, 0.0).astype(BF16)
    a_cs = sum(_dot(tri, part) for part in _split_bf16(da, 3))
    a_cs_t = jnp.concatenate([a_cs, jnp.zeros_like(a_cs)], axis=1).T

    dt_x = _expand_heads(dt, e_ref, 2)
    acs_x = _expand_heads(a_cs, e_ref, 3)
    ea_x = jnp.exp(acs_x)
    dte_x = jnp.exp(acs_x[q - 1:q, :] - acs_x)
    xdt = xs * dt_x
    xdt_b = xdt.astype(BF16)
    xw_b = (xdt * dte_x).astype(BF16)
    cd_x = ea_x[q - 1:q, :]

    gw = SSD_GROUP_WIDTH
    for g in range(SSD_GROUPS):
        bm = bc[:, g * SSD_STATE:(g + 1) * SSD_STATE]
        cm_b = bc[:, (SSD_GROUPS + g) * SSD_STATE:(SSD_GROUPS + g + 1) * SSD_STATE].astype(BF16)
        cb = _dot_nt(cm_b, bm.astype(BF16))
        st = state_ref[g]
        y_ref[:, g * gw:(g + 1) * gw] = _dot(cm_b, st.astype(BF16)) * ea_x[:, g * gw:(g + 1) * gw]
        for r in range(SSD_HEADS_PER_GROUP):
            h = g * SSD_HEADS_PER_GROUP + r
            seg = a_cs[:, h:h + 1] - a_cs_t[h:h + 1, :]
            decay = jnp.exp(jnp.where(causal, seg, -jnp.inf))
            lo, hi = h * SSD_HEAD_DIM, (h + 1) * SSD_HEAD_DIM
            y_ref[:, lo:hi] += _dot((cb * decay).astype(BF16), xdt_b[:, lo:hi])
        upd = _dot(bm.T.astype(BF16), xw_b[:, g * gw:(g + 1) * gw])
        state_ref[g] = st * cd_x[:, g * gw:(g + 1) * gw] + upd

    z = z_ref[...].astype(F32)
    y = (y_ref[...] + xs * dskip_ref[...]) * (z * _sigmoid(z))
    ms = jnp.mean(y * y, axis=-1, keepdims=True)
    o_ref[...] = (y * lax.rsqrt(ms + NORM_EPS) * ng_ref[...]).astype(o_ref.dtype)


def _ssd(proj, narrow, conv_w, conv_b, dt_bias, a_log, d_skip, norm_g, *, bsz, seq):
    q = SSD_CHUNK
    nc = seq // q
    one_hot = np.repeat(np.eye(SSD_HEADS, dtype=np.float32), SSD_HEAD_DIM, axis=1)
    expand = jnp.asarray(np.concatenate([one_hot] * 3 + [np.zeros_like(one_hot)], axis=0), BF16)
    shift = np.zeros((SSD_CONV * q, SSD_STAGE_ROWS), np.float32)
    t = np.arange(q)
    for j in range(SSD_CONV):
        shift[(t // 8) * (8 * SSD_CONV) + j * 8 + t % 8, np.where(t >= j, t - j, q + SSD_TAIL + t - j)] = 1.0
    shift = jnp.asarray(shift, BF16)
    row = lambda b, c: (b * nc + c)
    const = lambda b, c: (0, 0)
    return pl.pallas_call(
        _ssd_kernel,
        grid=(bsz, nc),
        in_specs=[pl.BlockSpec((q, SSD_INNER), lambda b, c: (row(b, c), P_Z // SSD_INNER)),
                  pl.BlockSpec((q, SSD_INNER), lambda b, c: (row(b, c), P_XS // SSD_INNER)),
                  pl.BlockSpec((q, SSD_BC), lambda b, c: (row(b, c), P_BC // SSD_BC)),
                  pl.BlockSpec((q, S_WIDTH), lambda b, c: (row(b, c), 0)),
                  pl.BlockSpec((SSD_CONV, SSD_INNER), const),
                  pl.BlockSpec((SSD_CONV, SSD_BC), const),
                  pl.BlockSpec((1, SSD_INNER), const),
                  pl.BlockSpec((1, SSD_BC), const),
                  pl.BlockSpec((1, SSD_HEADS), const),
                  pl.BlockSpec((1, SSD_HEADS), const),
                  pl.BlockSpec((1, SSD_INNER), const),
                  pl.BlockSpec((1, SSD_INNER), const),
                  pl.BlockSpec((4 * SSD_HEADS, SSD_INNER), const),
                  pl.BlockSpec((SSD_CONV * q, SSD_STAGE_ROWS), const)],
        out_specs=pl.BlockSpec((q, SSD_INNER), lambda b, c: (row(b, c), 0)),
        out_shape=jax.ShapeDtypeStruct((bsz * seq, SSD_INNER), BF16),
        scratch_shapes=[pltpu.VMEM((SSD_STAGE_ROWS, SSD_INNER), BF16),
                        pltpu.VMEM((SSD_STAGE_ROWS, SSD_BC), BF16),
                        pltpu.VMEM((SSD_GROUPS, SSD_STATE, SSD_GROUP_WIDTH), F32),
                        pltpu.VMEM((q, SSD_INNER), F32)],
        compiler_params=_cparams("parallel", "arbitrary"),
        name="ssd",
    )(proj, proj, proj, narrow,
      conv_w[:, :SSD_INNER], conv_w[:, SSD_INNER:], conv_b[None, :SSD_INNER], conv_b[None, SSD_INNER:],
      dt_bias[None, :], a_log[None, :], jnp.repeat(d_skip, SSD_HEAD_DIM)[None, :], norm_g[None, :], expand, shift)


def _regroup_kernel(w_ref, wide_ref, narrow_ref, *, step):
    def move(dst_ref, src, dst, width):
        for c in range(0, width, step):
            n = min(step, width - c)
            dst_ref[dst + c:dst + c + n, :] = w_ref[src + c:src + c + n, :].astype(dst_ref.dtype)

    for src, dst, width in _WIDE_PIECES:
        move(wide_ref, src, dst, width)
    used = S_WI + IDX_HEADS
    narrow_ref[used:, :] = jnp.zeros((S_WIDTH - used, narrow_ref.shape[1]), narrow_ref.dtype)
    for src, dst, width in _NARROW_PIECES:
        move(narrow_ref, src, dst, width)


def _regroup_w_in(w_in_t, *, cols=128):
    depth, n, d = w_in_t.shape
    return pl.pallas_call(
        functools.partial(_regroup_kernel, step=1024),
        grid=(depth, d // cols),
        in_specs=[pl.BlockSpec((None, n, cols), lambda l, c: (l, 0, c))],
        out_specs=[pl.BlockSpec((None, P_WIDTH, cols), lambda l, c: (l, 0, c)),
                   pl.BlockSpec((None, S_WIDTH, cols), lambda l, c: (l, 0, c))],
        out_shape=[jax.ShapeDtypeStruct((depth, P_WIDTH, d), BF16),
                   jax.ShapeDtypeStruct((depth, S_WIDTH, d), BF16)],
        compiler_params=_cparams("parallel", "parallel"),
        name="regroup_w_in",
    )(w_in_t)


def kernel(x, mem, positions, ffn1_norm, w_ffn1_in, w_ffn1_out, mix_norm, w_in, conv_w, conv_b, dt_bias, a_log,
           d_skip, ssd_norm, mem_norm, w_mem_kv, w_br_ssd, w_br_dsa, w_br_mem, w_out, ffn2_norm, w_ffn2_in,
           w_ffn2_out, final_norm):
    bsz, seq, d = x.shape
    depth = w_in.shape[0]
    assert d == D_MODEL and seq % DSA_QBLOCK == 0 and seq % SSD_CHUNK == 0 and mem.shape[1] == MEM_LEN

    w_p, w_n = _regroup_w_in(jnp.swapaxes(w_in, 1, 2))
    w1i, w1o = w_ffn1_in.astype(BF16), w_ffn1_out.astype(BF16)
    w2i, w2o = w_ffn2_in.astype(BF16), w_ffn2_out.astype(BF16)
    wkv = w_mem_kv.astype(BF16)
    wbs, wbd, wbm, wo = (w.astype(BF16) for w in (w_br_ssd, w_br_dsa, w_br_mem, w_out))

    tables = _rope_tables(positions)
    xf = x.reshape(bsz * seq, d)
    memf = mem.reshape(bsz * MEM_LEN, d)
    for l in range(depth):
        xf = _ffn(xf, ffn1_norm[l], w1i, w1o, final_norm, l, final_norm=False)
        proj = _norm_matmul(xf, mix_norm[l], w_p, l, tm=1024, tn=1024, out_dtype=BF16, name="in_proj",
                            w_is_transposed=True)
        narrow = _norm_matmul(xf, mix_norm[l], w_n, l, tm=1024, tn=S_WIDTH, out_dtype=F32, name="in_proj_narrow",
                              w_is_transposed=True)
        y_ssd = _ssd(proj, narrow, conv_w[l], conv_b[l], dt_bias[l], a_log[l], d_skip[l], ssd_norm[l],
                     bsz=bsz, seq=seq)
        y_dsa = _dsa(proj, narrow, tables, bsz=bsz, seq=seq)
        kv = _norm_matmul(memf, mem_norm[l], wkv, l, tm=1024, tn=512, out_dtype=BF16, name="mem_kv")
        y_mem = _memattn(proj, kv, bsz=bsz, seq=seq)
        merged = _branch_merge(y_ssd, y_dsa, y_mem, proj, wbs, wbd, wbm, l)
        xf = _out_proj(xf, merged, wo, l)
        xf = _ffn(xf, ffn2_norm[l], w2i, w2o, final_norm, l, final_norm=(l == depth - 1))
    return xf.reshape(bsz, seq, d)
```

```python
import functools
import math

import numpy as np
import jax
import jax.numpy as jnp
from jax import lax
from jax.experimental import pallas as pl
from jax.experimental.pallas import tpu as pltpu

F32 = jnp.float32
BF16 = jnp.bfloat16

D_MODEL = 2048
MEM_LEN = 256
NORM_EPS = 1e-6
ROPE_THETA = 500000.0
D_FF = 5632

SSD_INNER = 4096
SSD_HEAD_DIM = 64
SSD_HEADS = 64
SSD_GROUPS = 8
SSD_HEADS_PER_GROUP = 8
SSD_STATE = 128
SSD_CONV = 4
SSD_CHUNK = 128
SSD_GROUP_WIDTH = SSD_HEADS_PER_GROUP * SSD_HEAD_DIM
SSD_BC = 2 * SSD_GROUPS * SSD_STATE
SSD_TAIL = 16
SSD_STAGE_ROWS = 256

DSA_HEADS = 16
DSA_HEAD_DIM = 128
DSA_WIDTH = 2048
DSA_ROPE = 32
IDX_HEADS = 16
IDX_DIM = 64
IDX_ROPE = 16
DSA_TOPK_MAX = 256
DSA_QBLOCK = 128

XA_HEADS = 4
XA_HEAD_DIM = 512
XA_WIDTH = 2048

_IN_SPLITS = (SSD_INNER, SSD_INNER + SSD_BC, SSD_HEADS, DSA_WIDTH, DSA_HEAD_DIM, DSA_HEAD_DIM,
              IDX_HEADS * IDX_DIM, IDX_DIM, IDX_HEADS, XA_WIDTH, 3 * D_MODEL)
_IN_OFF = np.concatenate([[0], np.cumsum(_IN_SPLITS)]).tolist()

P_Z = 0
P_XS = 4096
P_Q = 8192
P_QMEM = 10240
P_GATE = 12288
P_BC = 18432
P_QI = 20480
P_WIDTH = 21504
S_K, S_V, S_KI, S_DT, S_WI = 0, 128, 256, 320, 384
S_WIDTH = 512
_O = _IN_OFF
_WIDE_PIECES = ((_O[0], P_Z, SSD_INNER), (_O[1], P_XS, SSD_INNER), (_O[3], P_Q, DSA_WIDTH),
                (_O[9], P_QMEM, XA_WIDTH), (_O[10], P_GATE, 3 * D_MODEL), (_O[1] + SSD_INNER, P_BC, SSD_BC),
                (_O[6], P_QI, IDX_HEADS * IDX_DIM))
_NARROW_PIECES = ((_O[4], S_K, DSA_HEAD_DIM), (_O[5], S_V, DSA_HEAD_DIM), (_O[7], S_KI, IDX_DIM),
                  (_O[2], S_DT, SSD_HEADS), (_O[8], S_WI, IDX_HEADS))

LANES = 128
V7X_VMEM_LIMIT = 56 * 1024 * 1024
INT_MIN = -(2 ** 31)
HALF_BIAS = 2 ** 15


def _cparams(*sem):
    return pltpu.CompilerParams(dimension_semantics=sem, vmem_limit_bytes=V7X_VMEM_LIMIT)


def _dot(a, b):
    return jnp.dot(a, b, preferred_element_type=F32)


def _dot_nt(a, b):
    return lax.dot_general(a, b, (((1,), (1,)), ((), ())), preferred_element_type=F32)


def _sigmoid(x):
    return 1.0 / (1.0 + jnp.exp(-x))


def _rms_rows_to(x_ref, g_ref, h_ref, rows):
    n = x_ref.shape[0] // rows

    def body(r, carry):
        sl = pl.ds(pl.multiple_of(r * rows, rows), rows)
        x = x_ref[sl, :]
        ms = jnp.mean(x * x, axis=-1, keepdims=True)
        h_ref[sl, :] = (x * lax.rsqrt(ms + NORM_EPS) * g_ref[...]).astype(h_ref.dtype)
        return carry

    lax.fori_loop(0, n, body, 0)


def _norm_matmul_kernel(x_ref, g_ref, w_ref, o_ref, h_ref, *, w_is_transposed):
    @pl.when(pl.program_id(1) == 0)
    def _():
        _rms_rows_to(x_ref, g_ref, h_ref, min(128, x_ref.shape[0]))

    mm = _dot_nt if w_is_transposed else _dot
    o_ref[...] = mm(h_ref[...], w_ref[...]).astype(o_ref.dtype)


def _norm_matmul(x, g, w, layer, *, tm, tn, out_dtype, name, w_is_transposed=False):
    m, d = x.shape
    n = w.shape[1] if w_is_transposed else w.shape[2]
    tm = min(tm, m)
    if w_is_transposed:
        w_spec = pl.BlockSpec((None, tn, d), lambda i, j: (layer, j, 0))
    else:
        w_spec = pl.BlockSpec((None, d, tn), lambda i, j: (layer, 0, j))
    return pl.pallas_call(
        functools.partial(_norm_matmul_kernel, w_is_transposed=w_is_transposed),
        grid=(m // tm, n // tn),
        in_specs=[pl.BlockSpec((tm, d), lambda i, j: (i, 0)),
                  pl.BlockSpec((1, d), lambda i, j: (0, 0)),
                  w_spec],
        out_specs=pl.BlockSpec((tm, tn), lambda i, j: (i, j)),
        out_shape=jax.ShapeDtypeStruct((m, n), out_dtype),
        scratch_shapes=[pltpu.VMEM((tm, d), BF16)],
        compiler_params=_cparams("parallel", "arbitrary"),
        name=name,
    )(x, g.reshape(1, d), w)


def _ffn_kernel(x_ref, g_ref, wg_ref, wu_ref, wo_ref, fg_ref, o_ref, h_ref, *, final_norm):
    j = pl.program_id(1)

    @pl.when(j == 0)
    def _():
        _rms_rows_to(x_ref, g_ref, h_ref, min(128, x_ref.shape[0]))
        o_ref[...] = x_ref[...]

    h = h_ref[...]
    gate = _dot(h, wg_ref[...])
    up = _dot(h, wu_ref[...])
    act = (gate * _sigmoid(gate) * up).astype(BF16)
    o_ref[...] += 0.5 * _dot(act, wo_ref[...])

    if final_norm:
        @pl.when(j == pl.num_programs(1) - 1)
        def _():
            _rms_rows_to(o_ref, fg_ref, o_ref, min(128, o_ref.shape[0]))


def _ffn(x, g, w_in, w_out, final_g, layer, *, final_norm, tm=1024, tf=512):
    m, d = x.shape
    f = w_out.shape[1]
    tm = min(tm, m)
    nf = f // tf
    return pl.pallas_call(
        functools.partial(_ffn_kernel, final_norm=final_norm),
        grid=(m // tm, nf),
        in_specs=[pl.BlockSpec((tm, d), lambda i, j: (i, 0)),
                  pl.BlockSpec((1, d), lambda i, j: (0, 0)),
                  pl.BlockSpec((None, d, tf), lambda i, j: (layer, 0, j)),
                  pl.BlockSpec((None, d, tf), lambda i, j: (layer, 0, j + nf)),
                  pl.BlockSpec((None, tf, d), lambda i, j: (layer, j, 0)),
                  pl.BlockSpec((1, d), lambda i, j: (0, 0))],
        out_specs=pl.BlockSpec((tm, d), lambda i, j: (i, 0)),
        out_shape=jax.ShapeDtypeStruct((m, d), F32),
        scratch_shapes=[pltpu.VMEM((tm, d), BF16)],
        compiler_params=_cparams("parallel", "arbitrary"),
        name="ffn_final" if final_norm else "ffn",
    )(x, g.reshape(1, d), w_in, w_in, w_out, final_g.reshape(1, d))


def _branch_merge_kernel(ys_ref, yd_ref, ym_ref, g0_ref, g1_ref, g2_ref, ws_ref, wd_ref, wm_ref, o_ref):
    merged = (_sigmoid(g0_ref[...].astype(F32)) * _dot(ys_ref[...], ws_ref[...])
              + _sigmoid(g1_ref[...].astype(F32)) * _dot(yd_ref[...], wd_ref[...])
              + _sigmoid(g2_ref[...].astype(F32)) * _dot(ym_ref[...], wm_ref[...]))
    o_ref[...] = merged.astype(o_ref.dtype)


def _branch_merge(y_ssd, y_dsa, y_mem, proj, w_s, w_d, w_m, layer, *, tm=1024, tn=256):
    m = y_ssd.shape[0]
    d = D_MODEL
    tm = min(tm, m)
    gate_blk = P_GATE // tn
    per_branch = d // tn
    return pl.pallas_call(
        _branch_merge_kernel,
        grid=(m // tm, d // tn),
        in_specs=[pl.BlockSpec((tm, SSD_INNER), lambda i, j: (i, 0)),
                  pl.BlockSpec((tm, DSA_WIDTH), lambda i, j: (i, 0)),
                  pl.BlockSpec((tm, XA_WIDTH), lambda i, j: (i, 0)),
                  pl.BlockSpec((tm, tn), lambda i, j: (i, gate_blk + j)),
                  pl.BlockSpec((tm, tn), lambda i, j: (i, gate_blk + per_branch + j)),
                  pl.BlockSpec((tm, tn), lambda i, j: (i, gate_blk + 2 * per_branch + j)),
                  pl.BlockSpec((None, SSD_INNER, tn), lambda i, j: (layer, 0, j)),
                  pl.BlockSpec((None, DSA_WIDTH, tn), lambda i, j: (layer, 0, j)),
                  pl.BlockSpec((None, XA_WIDTH, tn), lambda i, j: (layer, 0, j))],
        out_specs=pl.BlockSpec((tm, tn), lambda i, j: (i, j)),
        out_shape=jax.ShapeDtypeStruct((m, d), BF16),
        compiler_params=_cparams("parallel", "parallel"),
        name="branch_merge",
    )(y_ssd, y_dsa, y_mem, proj, proj, proj, w_s, w_d, w_m)


def _out_proj_kernel(x_ref, m_ref, w_ref, o_ref):
    o_ref[...] = x_ref[...] + _dot(m_ref[...], w_ref[...])


def _out_proj(x, merged, w_o, layer, *, tm=512):
    m, d = x.shape
    tm = min(tm, m)
    return pl.pallas_call(
        _out_proj_kernel,
        grid=(m // tm,),
        in_specs=[pl.BlockSpec((tm, d), lambda i: (i, 0)),
                  pl.BlockSpec((tm, d), lambda i: (i, 0)),
                  pl.BlockSpec((None, d, d), lambda i: (layer, 0, 0))],
        out_specs=pl.BlockSpec((tm, d), lambda i: (i, 0)),
        out_shape=jax.ShapeDtypeStruct((m, d), F32),
        compiler_params=_cparams("parallel"),
        name="out_proj",
    )(x, merged, w_o)


def _memattn_kernel(q_ref, k_ref, v_ref, o_ref, *, rows):
    k = k_ref[...]
    v = v_ref[...]
    scale = XA_HEAD_DIM ** -0.5

    for r in range(q_ref.shape[0] // rows):
        sl = slice(r * rows, (r + 1) * rows)
        logits = _dot_nt(q_ref[sl, :], k) * scale
        mx = jnp.max(logits, axis=-1, keepdims=True)
        e = jnp.exp(logits - mx)
        p = e * (1.0 / jnp.sum(e, axis=-1, keepdims=True))
        o_ref[sl, :] = _dot(p.astype(BF16), v).astype(o_ref.dtype)


def _memattn(proj, kv, *, bsz, seq):
    qblk = P_QMEM // XA_HEAD_DIM
    return pl.pallas_call(
        functools.partial(_memattn_kernel, rows=min(1024, seq)),
        grid=(bsz, XA_HEADS),
        in_specs=[pl.BlockSpec((seq, XA_HEAD_DIM), lambda b, h: (b, qblk + h)),
                  pl.BlockSpec((MEM_LEN, XA_HEAD_DIM), lambda b, h: (b, h)),
                  pl.BlockSpec((MEM_LEN, XA_HEAD_DIM), lambda b, h: (b, XA_HEADS + h))],
        out_specs=pl.BlockSpec((seq, XA_HEAD_DIM), lambda b, h: (b, h)),
        out_shape=jax.ShapeDtypeStruct((bsz * seq, XA_WIDTH), BF16),
        compiler_params=_cparams("parallel", "parallel"),
        name="memattn",
    )(proj, kv, kv)


def _rope_tables_kernel(pos_ref, ca_ref, sa_ref, ci_ref, si_ref, *, rows):
    lane = lax.broadcasted_iota(jnp.int32, (1, LANES), 1)
    ln_theta = math.log(ROPE_THETA)
    inv_a = jnp.exp((lane & (DSA_ROPE // 2 - 1)).astype(F32) * (-2.0 * ln_theta / DSA_ROPE))
    inv_i = jnp.exp((lane & (IDX_ROPE // 2 - 1)).astype(F32) * (-2.0 * ln_theta / IDX_ROPE))
    lane_i = lane & (IDX_DIM - 1)

    def body(r, carry):
        sl = pl.ds(pl.multiple_of(r * rows, rows), rows)
        pos = pos_ref[sl, :].astype(F32)
        ang = pos * inv_a
        c, s = jnp.cos(ang), jnp.sin(ang)
        ca_ref[sl, :] = jnp.where(lane < DSA_ROPE, c, 1.0)
        sa_ref[sl, :] = jnp.where(lane < DSA_ROPE // 2, -s, jnp.where(lane < DSA_ROPE, s, 0.0))
        ang = pos * inv_i
        c, s = jnp.cos(ang), jnp.sin(ang)
        ci_ref[sl, :] = jnp.where(lane_i < IDX_ROPE, c, 1.0)
        si_ref[sl, :] = jnp.where(lane_i < IDX_ROPE // 2, -s, jnp.where(lane_i < IDX_ROPE, s, 0.0))
        return carry

    lax.fori_loop(0, pos_ref.shape[0] // rows, body, 0)


def _rope_tables(positions):
    bsz, seq = positions.shape
    tab = jax.ShapeDtypeStruct((bsz * seq, LANES), F32)
    spec = pl.BlockSpec((seq, LANES), lambda b: (b, 0))
    return pl.pallas_call(
        functools.partial(_rope_tables_kernel, rows=64),
        grid=(bsz,),
        in_specs=[pl.BlockSpec((seq, 1), lambda b: (b, 0))],
        out_specs=[spec, spec, spec, spec],
        out_shape=[tab, tab, tab, tab],
        compiler_params=_cparams("parallel"),
        name="rope_tables",
    )(positions.reshape(bsz * seq, 1))


def _swap_attn(x, lane):
    h = DSA_ROPE // 2
    return jnp.where(lane < h, pltpu.roll(x, LANES - h, 1), pltpu.roll(x, h, 1))


def _swap_idx(x, lane_i):
    h = IDX_ROPE // 2
    return jnp.where(lane_i < h, pltpu.roll(x, LANES - h, 1), pltpu.roll(x, h, 1))


def _tree_sum(parts):
    while len(parts) > 1:
        parts = [parts[a] + parts[a + 1] for a in range(0, len(parts) - 1, 2)] + ([parts[-1]] if len(parts) % 2 else [])
    return parts[0]


def _dsa_kernel(q_ref, qi_ref, sm_ref, ca_ref, sa_ref, ci_ref, si_ref, rota_ref, roti_ref, o_ref,
                k_ref, kil_ref, kih_ref, v_ref, xr_ref, key_ref, half_ref, lim_ref, bias_ref, qs_ref, l_ref,
                m_ref, s_ref, acc_ref, *, topk, tq, kc):
    qb = pl.program_id(1)
    seq = sm_ref.shape[0]
    lane = lax.broadcasted_iota(jnp.int32, (1, LANES), 1)
    lane_i = lane & (IDX_DIM - 1)

    @pl.when(qb == 0)
    def _prep():
        rows = min(256, seq)

        def body(r, carry):
            sl = pl.ds(pl.multiple_of(r * rows, rows), rows)
            k = sm_ref[sl, S_K:S_K + LANES]
            k_ref[sl, :] = (k * ca_ref[sl, :] + _swap_attn(k, lane) * sa_ref[sl, :]).astype(BF16)
            v_ref[sl, :] = sm_ref[sl, S_V:S_V + LANES].astype(BF16)
            ki = sm_ref[sl, S_KI:S_KI + LANES]
            kir = ki * ci_ref[sl, :] + _swap_idx(ki, lane_i) * si_ref[sl, :]
            kil = jnp.where(lane < IDX_DIM, kir, 0.0)
            kil_ref[sl, :] = kil.astype(BF16)
            kih_ref[sl, :] = pltpu.roll(kil, IDX_DIM, 1).astype(BF16)
            return carry

        lax.fori_loop(0, seq // rows, body, 0)

    row0 = pl.multiple_of(qb * tq, tq)
    qrows = pl.ds(row0, tq)
    ca_q, sa_q = ca_ref[qrows, :], sa_ref[qrows, :]
    ci_q, si_q = ci_ref[qrows, :], si_ref[qrows, :]

    n_chunks = (row0 + tq + kc - 1) // kc

    def chunk_rows(c):
        return pl.ds(pl.multiple_of(c * kc, kc), kc)

    w_t = sm_ref[qrows, S_WI:S_WI + LANES].T * (IDX_DIM ** -0.5 * IDX_HEADS ** -0.5)

    def rotate(x, rot_ref, cos, sin):
        both = _dot(x, rot_ref[...])
        return (both[:, 0:LANES] * cos + both[:, LANES:2 * LANES] * sin).astype(BF16)

    for pr in range(IDX_HEADS // 2):
        xr_ref[pr] = rotate(qi_ref[:, pr * LANES:(pr + 1) * LANES], roti_ref, ci_q, si_q)
    qpos = row0 + lax.broadcasted_iota(jnp.int32, (kc, tq), 1)
    kpos0 = lax.broadcasted_iota(jnp.int32, (kc, tq), 0)

    def index_chunk(c, carry):
        ks = chunk_rows(c)
        kil, kih = kil_ref[ks, :], kih_ref[ks, :]
        score = jnp.zeros((kc, tq), F32)
        for pr in range(IDX_HEADS // 2):
            xr = xr_ref[pr]
            score = score + (jnp.maximum(_dot_nt(kil, xr), 0.0) * w_t[2 * pr:2 * pr + 1, :]
                             + jnp.maximum(_dot_nt(kih, xr), 0.0) * w_t[2 * pr + 1:2 * pr + 2, :])
        bits = lax.bitcast_convert_type(score, jnp.int32)
        skey = jnp.where(bits >= 0, bits, bits ^ jnp.int32(0x7FFFFFFF))
        key = jnp.where(kpos0 + c * kc <= qpos, skey, jnp.int32(INT_MIN))
        key_ref[ks, :] = key
        half_ref[ks, :] = lax.shift_right_arithmetic(key, 16).astype(jnp.int16)
        return carry

    lax.fori_loop(0, n_chunks, index_chunk, 0)

    def count_halves(pred):
        def count_chunk(c, acc):
            hit = jnp.where(pred(half_ref[chunk_rows(c), :]), jnp.int16(1), jnp.int16(0))
            return acc + _tree_sum([hit[j * 16:(j + 1) * 16, :] for j in range(kc // 16)])

        acc = lax.fori_loop(0, n_chunks, count_chunk, jnp.zeros((16, tq), jnp.int16))
        return jnp.sum(acc.astype(F32), axis=0, keepdims=True)

    def search_half(wanted):
        def step(i, u):
            cand_u = u + lax.shift_left(jnp.int32(1), 15 - i)
            cand = (cand_u - HALF_BIAS).astype(jnp.int16)
            return jnp.where(count_halves(lambda h: h >= cand) >= wanted, cand_u, u)

        return lax.fori_loop(0, 16, step, jnp.zeros((1, tq), jnp.int32)) - HALF_BIAS

    def count_keys(pred, also=None):
        def count_chunk(c, acc):
            k, pos = key_ref[chunk_rows(c), :], kpos0 + c * kc
            hit = jnp.where(pred(k, pos), 1.0, 0.0)
            if also is not None:
                hit = jnp.where(also(k, pos), hit, 0.0)
            return acc + _tree_sum([hit[j * 8:(j + 1) * 8, :] for j in range(kc // 8)])

        acc = lax.fori_loop(0, n_chunks, count_chunk, jnp.zeros((8, tq), F32))
        return jnp.sum(acc, axis=0, keepdims=True)

    thr_hi = search_half(float(topk))
    thr_hi16 = thr_hi.astype(jnp.int16)
    wanted_lo = float(topk) - count_halves(lambda h: h > thr_hi16)

    def low_halves(c, carry):
        ks = chunk_rows(c)
        low = ((key_ref[ks, :] & jnp.int32(0xFFFF)) - HALF_BIAS).astype(jnp.int16)
        half_ref[ks, :] = jnp.where(half_ref[ks, :] == thr_hi16, low, jnp.int16(-HALF_BIAS))
        return carry

    lax.fori_loop(0, n_chunks, low_halves, 0)
    thr = lax.shift_left(thr_hi, 16) | (search_half(wanted_lo) + HALF_BIAS)

    need = float(topk) - count_keys(lambda k, pos: k > thr)
    n_tied = count_keys(lambda k, pos: k == thr)
    lim_ref[...] = jnp.full(lim_ref.shape, seq, jnp.int32)
    surplus = jnp.max(jnp.where(thr > INT_MIN, n_tied - need, 0.0))

    @pl.when(surplus > 0.0)
    def _():
        nbits = (seq - 1).bit_length()

        def refine(i, lo):
            cand = lo + lax.shift_left(jnp.int32(1), nbits - 1 - i)
            below = count_keys(lambda k, pos: k == thr, lambda k, pos: pos < cand)
            return jnp.where(below < need, cand, lo)

        lo = lax.fori_loop(0, nbits, refine, jnp.zeros((1, tq), jnp.int32))
        lim_ref[...] = jnp.broadcast_to(lo + 1, lim_ref.shape)

    def bias_chunk(c, carry):
        k = key_ref[chunk_rows(c), :]
        pos = kpos0 + c * kc
        visible = jnp.where(pos <= qpos, 0.0, -jnp.inf)
        tied = jnp.where(k == thr, jnp.where(pos < lim_ref[0:1, :], visible, -jnp.inf), -jnp.inf)
        bias = jnp.where(k > thr, visible, tied)
        for kb in range(kc // LANES):
            bias_ref[c, :, kb * LANES:(kb + 1) * LANES] = bias[kb * LANES:(kb + 1) * LANES, :].T
        return carry

    lax.fori_loop(0, n_chunks, bias_chunk, 0)

    for h in range(DSA_HEADS):
        qs_ref[h * tq:(h + 1) * tq, :] = rotate(q_ref[:, h * LANES:(h + 1) * LANES], rota_ref, ca_q, sa_q)

    scale2 = DSA_HEAD_DIM ** -0.5 * math.log2(math.e)

    def logits_chunk(c, first):
        lg = _dot_nt(qs_ref[...], k_ref[chunk_rows(c), :]) * scale2
        lg = lg + jnp.concatenate([bias_ref[c]] * DSA_HEADS, axis=0)
        l_ref[c] = lg
        part = lg[:, 0:LANES]
        for kb in range(1, kc // LANES):
            part = jnp.maximum(part, lg[:, kb * LANES:(kb + 1) * LANES])
        m_ref[...] = part if first else jnp.maximum(m_ref[...], part)

    logits_chunk(0, True)
    lax.fori_loop(1, n_chunks, lambda c, carry: (logits_chunk(c, False), carry)[1], 0)
    m_ref[...] = jnp.broadcast_to(jnp.max(m_ref[...], axis=-1, keepdims=True), m_ref.shape)

    def value_chunk(c, first):
        e = jnp.exp2(l_ref[c] - jnp.concatenate([m_ref[...]] * (kc // LANES), axis=1))
        rowsum = _tree_sum([e[:, kb * LANES:(kb + 1) * LANES] for kb in range(kc // LANES)])
        out = _dot(e.astype(BF16), v_ref[chunk_rows(c), :])
        s_ref[...] = rowsum if first else s_ref[...] + rowsum
        acc_ref[...] = out if first else acc_ref[...] + out

    value_chunk(0, True)
    lax.fori_loop(1, n_chunks, lambda c, carry: (value_chunk(c, False), carry)[1], 0)
    for h in range(DSA_HEADS):
        hs = slice(h * tq, (h + 1) * tq)
        inv = 1.0 / jnp.sum(s_ref[hs, :], axis=-1, keepdims=True)
        o_ref[:, h * LANES:(h + 1) * LANES] = (acc_ref[hs, :] * inv).astype(o_ref.dtype)


def _dsa(proj, narrow, tables, *, bsz, seq):
    tq = DSA_QBLOCK
    kc = min(256, seq)
    topk = min(DSA_TOPK_MAX, seq // 4)
    nq = seq // tq
    tab_spec = pl.BlockSpec((seq, LANES), lambda b, i: (b, 0))
    rot_spec = pl.BlockSpec((LANES, 2 * LANES), lambda b, i: (0, 0))

    def partner_matrix(period, half):
        p = np.zeros((LANES, LANES), np.float32)
        j = np.arange(LANES)
        first, second = (j % period) < half, ((j % period) >= half) & ((j % period) < 2 * half)
        p[j[first] + half, j[first]] = 1.0
        p[j[second] - half, j[second]] = 1.0
        return jnp.asarray(np.concatenate([np.eye(LANES, dtype=np.float32), p], axis=1), BF16)

    rot_a = partner_matrix(LANES, DSA_ROPE // 2)
    rot_i = partner_matrix(IDX_DIM, IDX_ROPE // 2)
    return pl.pallas_call(
        functools.partial(_dsa_kernel, topk=topk, tq=tq, kc=kc),
        grid=(bsz, nq),
        in_specs=[pl.BlockSpec((tq, DSA_WIDTH), lambda b, i: (b * nq + i, P_Q // DSA_WIDTH)),
                  pl.BlockSpec((tq, IDX_HEADS * IDX_DIM), lambda b, i: (b * nq + i, P_QI // (IDX_HEADS * IDX_DIM))),
                  pl.BlockSpec((seq, S_WIDTH), lambda b, i: (b, 0)),
                  tab_spec, tab_spec, tab_spec, tab_spec, rot_spec, rot_spec],
        out_specs=pl.BlockSpec((tq, DSA_WIDTH), lambda b, i: (b * nq + i, 0)),
        out_shape=jax.ShapeDtypeStruct((bsz * seq, DSA_WIDTH), BF16),
        scratch_shapes=[pltpu.VMEM((seq, LANES), BF16),
                        pltpu.VMEM((seq, LANES), BF16),
                        pltpu.VMEM((seq, LANES), BF16),
                        pltpu.VMEM((seq, LANES), BF16),
                        pltpu.VMEM((IDX_HEADS // 2, tq, LANES), BF16),
                        pltpu.VMEM((seq, tq), jnp.int32),
                        pltpu.VMEM((seq, tq), jnp.int16),
                        pltpu.VMEM((8, tq), jnp.int32),
                        pltpu.VMEM((seq // kc, tq, kc), F32),
                        pltpu.VMEM((DSA_HEADS * tq, LANES), BF16),
                        pltpu.VMEM((seq // kc, DSA_HEADS * tq, kc), F32),
                        pltpu.VMEM((DSA_HEADS * tq, LANES), F32),
                        pltpu.VMEM((DSA_HEADS * tq, LANES), F32),
                        pltpu.VMEM((DSA_HEADS * tq, LANES), F32)],
        compiler_params=_cparams("parallel", "arbitrary"),
        name="dsa",
    )(proj, proj, narrow, *tables, rot_a, rot_i)


def _split_bf16(x, parts):
    out = []
    for _ in range(parts - 1):
        hi = x.astype(BF16)
        out.append(hi)
        x = x - hi.astype(F32)
    out.append(x.astype(BF16))
    return out


def _expand_heads(x, e_ref, parts):
    pieces = [p.astype(F32) for p in _split_bf16(x, parts)]
    k = parts * SSD_HEADS
    k_pad = -k % LANES
    if k_pad:
        pieces.append(jnp.zeros((x.shape[0], k_pad), F32))
    lhs = jnp.concatenate(pieces, axis=1).astype(BF16)
    return _dot(lhs, e_ref[0:k + k_pad, :])


def _ssd_kernel(z_ref, xs_ref, bc_ref, sm_ref, cwx_ref, cwb_ref, cbx_ref, cbb_ref,
                dtb_ref, alog_ref, dskip_ref, ng_ref, e_ref, shift_ref, o_ref,
                xcat_ref, bcat_ref, state_ref, y_ref):
    c = pl.program_id(1)
    q = SSD_CHUNK
    tail = SSD_TAIL

    @pl.when(c == 0)
    def _():
        xcat_ref[q:, :] = jnp.zeros((xcat_ref.shape[0] - q, SSD_INNER), xcat_ref.dtype)
        bcat_ref[q:, :] = jnp.zeros((bcat_ref.shape[0] - q, SSD_BC), bcat_ref.dtype)
        state_ref[...] = jnp.zeros_like(state_ref)

    xcat_ref[0:q, :] = xs_ref[...]
    bcat_ref[0:q, :] = bc_ref[...]

    def conv_silu(cat_ref, w_ref, b_ref):
        width = cat_ref.shape[1]
        delayed = _dot(shift_ref[...], cat_ref[...]).reshape(q // 8, SSD_CONV, 8, width)
        acc = b_ref[...] + w_ref[SSD_CONV - 1:SSD_CONV, :] * delayed[:, 0]
        for j in range(1, SSD_CONV):
            acc = acc + w_ref[SSD_CONV - 1 - j:SSD_CONV - j, :] * delayed[:, j]
        acc = acc.reshape(q, width)
        return acc * _sigmoid(acc)

    xs = conv_silu(xcat_ref, cwx_ref, cbx_ref)
    bc = conv_silu(bcat_ref, cwb_ref, cbb_ref)
    xcat_ref[q:q + tail, :] = xcat_ref[q - tail:q, :]
    bcat_ref[q:q + tail, :] = bcat_ref[q - tail:q, :]

    pre = sm_ref[:, S_DT:S_DT + SSD_HEADS] + dtb_ref[...]
    dt = jnp.maximum(pre, 0.0) + jnp.log1p(jnp.exp(-jnp.abs(pre)))
    da = dt * (-jnp.exp(alog_ref[...]))
    li = lax.broadcasted_iota(jnp.int32, (q, q), 0)
    si = lax.broadcasted_iota(jnp.int32, (q, q), 1)
    causal = li >= si
    tri = jnp.where(causal, 1.0, 0.0).astype(BF16)
    a_cs = sum(_dot(tri, part) for part in _split_bf16(da, 3))
    a_cs_t = jnp.concatenate([a_cs, jnp.zeros_like(a_cs)], axis=1).T

    dt_x = _expand_heads(dt, e_ref, 2)
    acs_x = _expand_heads(a_cs, e_ref, 3)
    ea_x = jnp.exp(acs_x)
    dte_x = jnp.exp(acs_x[q - 1:q, :] - acs_x)
    xdt = xs * dt_x
    xdt_b = xdt.astype(BF16)
    xw_b = (xdt * dte_x).astype(BF16)
    cd_x = ea_x[q - 1:q, :]

    gw = SSD_GROUP_WIDTH
    for g in range(SSD_GROUPS):
        bm = bc[:, g * SSD_STATE:(g + 1) * SSD_STATE]
        cm_b = bc[:, (SSD_GROUPS + g) * SSD_STATE:(SSD_GROUPS + g + 1) * SSD_STATE].astype(BF16)
        cb = _dot_nt(cm_b, bm.astype(BF16))
        st = state_ref[g]
        y_ref[:, g * gw:(g + 1) * gw] = _dot(cm_b, st.astype(BF16)) * ea_x[:, g * gw:(g + 1) * gw]
        for r in range(SSD_HEADS_PER_GROUP):
            h = g * SSD_HEADS_PER_GROUP + r
            seg = a_cs[:, h:h + 1] - a_cs_t[h:h + 1, :]
            decay = jnp.exp(jnp.where(causal, seg, -jnp.inf))
            lo, hi = h * SSD_HEAD_DIM, (h + 1) * SSD_HEAD_DIM
            y_ref[:, lo:hi] += _dot((cb * decay).astype(BF16), xdt_b[:, lo:hi])
        upd = _dot(bm.T.astype(BF16), xw_b[:, g * gw:(g + 1) * gw])
        state_ref[g] = st * cd_x[:, g * gw:(g + 1) * gw] + upd

    z = z_ref[...].astype(F32)
    y = (y_ref[...] + xs * dskip_ref[...]) * (z * _sigmoid(z))
    ms = jnp.mean(y * y, axis=-1, keepdims=True)
    o_ref[...] = (y * lax.rsqrt(ms + NORM_EPS) * ng_ref[...]).astype(o_ref.dtype)


def _ssd(proj, narrow, conv_w, conv_b, dt_bias, a_log, d_skip, norm_g, *, bsz, seq):
    q = SSD_CHUNK
    nc = seq // q
    one_hot = np.repeat(np.eye(SSD_HEADS, dtype=np.float32), SSD_HEAD_DIM, axis=1)
    expand = jnp.asarray(np.concatenate([one_hot] * 3 + [np.zeros_like(one_hot)], axis=0), BF16)
    shift = np.zeros((SSD_CONV * q, SSD_STAGE_ROWS), np.float32)
    t = np.arange(q)
    for j in range(SSD_CONV):
        shift[(t // 8) * (8 * SSD_CONV) + j * 8 + t % 8, np.where(t >= j, t - j, q + SSD_TAIL + t - j)] = 1.0
    shift = jnp.asarray(shift, BF16)
    row = lambda b, c: (b * nc + c)
    const = lambda b, c: (0, 0)
    return pl.pallas_call(
        _ssd_kernel,
        grid=(bsz, nc),
        in_specs=[pl.BlockSpec((q, SSD_INNER), lambda b, c: (row(b, c), P_Z // SSD_INNER)),
                  pl.BlockSpec((q, SSD_INNER), lambda b, c: (row(b, c), P_XS // SSD_INNER)),
                  pl.BlockSpec((q, SSD_BC), lambda b, c: (row(b, c), P_BC // SSD_BC)),
                  pl.BlockSpec((q, S_WIDTH), lambda b, c: (row(b, c), 0)),
                  pl.BlockSpec((SSD_CONV, SSD_INNER), const),
                  pl.BlockSpec((SSD_CONV, SSD_BC), const),
                  pl.BlockSpec((1, SSD_INNER), const),
                  pl.BlockSpec((1, SSD_BC), const),
                  pl.BlockSpec((1, SSD_HEADS), const),
                  pl.BlockSpec((1, SSD_HEADS), const),
                  pl.BlockSpec((1, SSD_INNER), const),
                  pl.BlockSpec((1, SSD_INNER), const),
                  pl.BlockSpec((4 * SSD_HEADS, SSD_INNER), const),
                  pl.BlockSpec((SSD_CONV * q, SSD_STAGE_ROWS), const)],
        out_specs=pl.BlockSpec((q, SSD_INNER), lambda b, c: (row(b, c), 0)),
        out_shape=jax.ShapeDtypeStruct((bsz * seq, SSD_INNER), BF16),
        scratch_shapes=[pltpu.VMEM((SSD_STAGE_ROWS, SSD_INNER), BF16),
                        pltpu.VMEM((SSD_STAGE_ROWS, SSD_BC), BF16),
                        pltpu.VMEM((SSD_GROUPS, SSD_STATE, SSD_GROUP_WIDTH), F32),
                        pltpu.VMEM((q, SSD_INNER), F32)],
        compiler_params=_cparams("parallel", "arbitrary"),
        name="ssd",
    )(proj, proj, proj, narrow,
      conv_w[:, :SSD_INNER], conv_w[:, SSD_INNER:], conv_b[None, :SSD_INNER], conv_b[None, SSD_INNER:],
      dt_bias[None, :], a_log[None, :], jnp.repeat(d_skip, SSD_HEAD_DIM)[None, :], norm_g[None, :], expand, shift)


def _regroup_kernel(w_ref, wide_ref, narrow_ref, *, step):
    def move(dst_ref, src, dst, width):
        for c in range(0, width, step):
            n = min(step, width - c)
            dst_ref[dst + c:dst + c + n, :] = w_ref[src + c:src + c + n, :].astype(dst_ref.dtype)

    for src, dst, width in _WIDE_PIECES:
        move(wide_ref, src, dst, width)
    used = S_WI + IDX_HEADS
    narrow_ref[used:, :] = jnp.zeros((S_WIDTH - used, narrow_ref.shape[1]), narrow_ref.dtype)
    for src, dst, width in _NARROW_PIECES:
        move(narrow_ref, src, dst, width)


def _regroup_w_in(w_in_t, *, cols=128):
    depth, n, d = w_in_t.shape
    return pl.pallas_call(
        functools.partial(_regroup_kernel, step=1024),
        grid=(depth, d // cols),
        in_specs=[pl.BlockSpec((None, n, cols), lambda l, c: (l, 0, c))],
        out_specs=[pl.BlockSpec((None, P_WIDTH, cols), lambda l, c: (l, 0, c)),
                   pl.BlockSpec((None, S_WIDTH, cols), lambda l, c: (l, 0, c))],
        out_shape=[jax.ShapeDtypeStruct((depth, P_WIDTH, d), BF16),
                   jax.ShapeDtypeStruct((depth, S_WIDTH, d), BF16)],
        compiler_params=_cparams("parallel", "parallel"),
        name="regroup_w_in",
    )(w_in_t)


def kernel(x, mem, positions, ffn1_norm, w_ffn1_in, w_ffn1_out, mix_norm, w_in, conv_w, conv_b, dt_bias, a_log,
           d_skip, ssd_norm, mem_norm, w_mem_kv, w_br_ssd, w_br_dsa, w_br_mem, w_out, ffn2_norm, w_ffn2_in,
           w_ffn2_out, final_norm):
    bsz, seq, d = x.shape
    depth = w_in.shape[0]
    assert d == D_MODEL and seq % DSA_QBLOCK == 0 and seq % SSD_CHUNK == 0 and mem.shape[1] == MEM_LEN

    w_p, w_n = _regroup_w_in(jnp.swapaxes(w_in, 1, 2))
    w1i, w1o = w_ffn1_in.astype(BF16), w_ffn1_out.astype(BF16)
    w2i, w2o = w_ffn2_in.astype(BF16), w_ffn2_out.astype(BF16)
    wkv = w_mem_kv.astype(BF16)
    wbs, wbd, wbm, wo = (w.astype(BF16) for w in (w_br_ssd, w_br_dsa, w_br_mem, w_out))

    tables = _rope_tables(positions)
    xf = x.reshape(bsz * seq, d)
    memf = mem.reshape(bsz * MEM_LEN, d)
    for l in range(depth):
        xf = _ffn(xf, ffn1_norm[l], w1i, w1o, final_norm, l, final_norm=False)
        proj = _norm_matmul(xf, mix_norm[l], w_p, l, tm=1024, tn=1024, out_dtype=BF16, name="in_proj",
                            w_is_transposed=True)
        narrow = _norm_matmul(xf, mix_norm[l], w_n, l, tm=1024, tn=S_WIDTH, out_dtype=F32, name="in_proj_narrow",
                              w_is_transposed=True)
        y_ssd = _ssd(proj, narrow, conv_w[l], conv_b[l], dt_bias[l], a_log[l], d_skip[l], ssd_norm[l],
                     bsz=bsz, seq=seq)
        y_dsa = _dsa(proj, narrow, tables, bsz=bsz, seq=seq)
        kv = _norm_matmul(memf, mem_norm[l], wkv, l, tm=1024, tn=512, out_dtype=BF16, name="mem_kv")
        y_mem = _memattn(proj, kv, bsz=bsz, seq=seq)
        merged = _branch_merge(y_ssd, y_dsa, y_mem, proj, wbs, wbd, wbm, l)
        xf = _out_proj(xf, merged, wo, l)
        xf = _ffn(xf, ffn2_norm[l], w2i, w2o, final_norm, l, final_norm=(l == depth - 1))
    return xf.reshape(bsz, seq, d)
```

```python
import functools
import math

import numpy as np
import jax
import jax.numpy as jnp
from jax import lax
from jax.experimental import pallas as pl
from jax.experimental.pallas import tpu as pltpu

F32 = jnp.float32
BF16 = jnp.bfloat16

D_MODEL = 2048
MEM_LEN = 256
NORM_EPS = 1e-6
ROPE_THETA = 500000.0
D_FF = 5632

SSD_INNER = 4096
SSD_HEAD_DIM = 64
SSD_HEADS = 64
SSD_GROUPS = 8
SSD_HEADS_PER_GROUP = 8
SSD_STATE = 128
SSD_CONV = 4
SSD_CHUNK = 128
SSD_GROUP_WIDTH = SSD_HEADS_PER_GROUP * SSD_HEAD_DIM
SSD_BC = 2 * SSD_GROUPS * SSD_STATE
SSD_TAIL = 16
SSD_STAGE_ROWS = 256

DSA_HEADS = 16
DSA_HEAD_DIM = 128
DSA_WIDTH = 2048
DSA_ROPE = 32
IDX_HEADS = 16
IDX_DIM = 64
IDX_ROPE = 16
DSA_TOPK_MAX = 256
DSA_QBLOCK = 128

XA_HEADS = 4
XA_HEAD_DIM = 512
XA_WIDTH = 2048

_IN_SPLITS = (SSD_INNER, SSD_INNER + SSD_BC, SSD_HEADS, DSA_WIDTH, DSA_HEAD_DIM, DSA_HEAD_DIM,
              IDX_HEADS * IDX_DIM, IDX_DIM, IDX_HEADS, XA_WIDTH, 3 * D_MODEL)
_IN_OFF = np.concatenate([[0], np.cumsum(_IN_SPLITS)]).tolist()

P_Z = 0
P_XS = 4096
P_Q = 8192
P_QMEM = 10240
P_GATE = 12288
P_BC = 18432
P_QI = 20480
P_WIDTH = 21504
S_K, S_V, S_KI, S_DT, S_WI = 0, 128, 256, 320, 384
S_WIDTH = 512
_O = _IN_OFF
_WIDE_PIECES = ((_O[0], P_Z, SSD_INNER), (_O[1], P_XS, SSD_INNER), (_O[3], P_Q, DSA_WIDTH),
                (_O[9], P_QMEM, XA_WIDTH), (_O[10], P_GATE, 3 * D_MODEL), (_O[1] + SSD_INNER, P_BC, SSD_BC),
                (_O[6], P_QI, IDX_HEADS * IDX_DIM))
_NARROW_PIECES = ((_O[4], S_K, DSA_HEAD_DIM), (_O[5], S_V, DSA_HEAD_DIM), (_O[7], S_KI, IDX_DIM),
                  (_O[2], S_DT, SSD_HEADS), (_O[8], S_WI, IDX_HEADS))

LANES = 128
V7X_VMEM_LIMIT = 56 * 1024 * 1024
INT_MIN = -(2 ** 31)


def _cparams(*sem):
    return pltpu.CompilerParams(dimension_semantics=sem, vmem_limit_bytes=V7X_VMEM_LIMIT)


def _dot(a, b):
    return jnp.dot(a, b, preferred_element_type=F32)


def _dot_nt(a, b):
    return lax.dot_general(a, b, (((1,), (1,)), ((), ())), preferred_element_type=F32)


def _sigmoid(x):
    return 1.0 / (1.0 + jnp.exp(-x))


def _rms_rows_to(x_ref, g_ref, h_ref, rows):
    n = x_ref.shape[0] // rows

    def body(r, carry):
        sl = pl.ds(pl.multiple_of(r * rows, rows), rows)
        x = x_ref[sl, :]
        ms = jnp.mean(x * x, axis=-1, keepdims=True)
        h_ref[sl, :] = (x * lax.rsqrt(ms + NORM_EPS) * g_ref[...]).astype(h_ref.dtype)
        return carry

    lax.fori_loop(0, n, body, 0)


def _norm_matmul_kernel(x_ref, g_ref, w_ref, o_ref, h_ref):
    @pl.when(pl.program_id(1) == 0)
    def _():
        _rms_rows_to(x_ref, g_ref, h_ref, min(128, x_ref.shape[0]))

    o_ref[...] = _dot(h_ref[...], w_ref[...]).astype(o_ref.dtype)


def _norm_matmul(x, g, w, layer, *, tm, tn, out_dtype, name):
    m, d = x.shape
    n = w.shape[2]
    tm = min(tm, m)
    return pl.pallas_call(
        _norm_matmul_kernel,
        grid=(m // tm, n // tn),
        in_specs=[pl.BlockSpec((tm, d), lambda i, j: (i, 0)),
                  pl.BlockSpec((1, d), lambda i, j: (0, 0)),
                  pl.BlockSpec((None, d, tn), lambda i, j: (layer, 0, j))],
        out_specs=pl.BlockSpec((tm, tn), lambda i, j: (i, j)),
        out_shape=jax.ShapeDtypeStruct((m, n), out_dtype),
        scratch_shapes=[pltpu.VMEM((tm, d), BF16)],
        compiler_params=_cparams("parallel", "arbitrary"),
        name=name,
    )(x, g.reshape(1, d), w)


def _in_proj_kernel(x_ref, g_ref, w_ref, wn_ref, o_ref, on_ref, h_ref):
    @pl.when(pl.program_id(1) == 0)
    def _():
        _rms_rows_to(x_ref, g_ref, h_ref, min(128, x_ref.shape[0]))
        on_ref[...] = _dot_nt(h_ref[...], wn_ref[...])

    o_ref[...] = _dot_nt(h_ref[...], w_ref[...]).astype(o_ref.dtype)


def _in_proj(x, g, w_wide, w_narrow, layer, *, tm=1024, tn=1024):
    m, d = x.shape
    tm = min(tm, m)
    return pl.pallas_call(
        _in_proj_kernel,
        grid=(m // tm, P_WIDTH // tn),
        in_specs=[pl.BlockSpec((tm, d), lambda i, j: (i, 0)),
                  pl.BlockSpec((1, d), lambda i, j: (0, 0)),
                  pl.BlockSpec((None, tn, d), lambda i, j: (layer, j, 0)),
                  pl.BlockSpec((None, S_WIDTH, d), lambda i, j: (layer, 0, 0))],
        out_specs=[pl.BlockSpec((tm, tn), lambda i, j: (i, j)),
                   pl.BlockSpec((tm, S_WIDTH), lambda i, j: (i, 0))],
        out_shape=[jax.ShapeDtypeStruct((m, P_WIDTH), BF16), jax.ShapeDtypeStruct((m, S_WIDTH), F32)],
        scratch_shapes=[pltpu.VMEM((tm, d), BF16)],
        compiler_params=_cparams("parallel", "arbitrary"),
        name="in_proj",
    )(x, g.reshape(1, d), w_wide, w_narrow)


def _ffn_kernel(x_ref, g_ref, wg_ref, wu_ref, wo_ref, fg_ref, o_ref, h_ref, *, final_norm):
    j = pl.program_id(1)

    @pl.when(j == 0)
    def _():
        _rms_rows_to(x_ref, g_ref, h_ref, min(128, x_ref.shape[0]))
        o_ref[...] = x_ref[...]

    h = h_ref[...]
    gate = _dot(h, wg_ref[...])
    up = _dot(h, wu_ref[...])
    act = (gate * _sigmoid(gate) * up).astype(BF16)
    o_ref[...] += 0.5 * _dot(act, wo_ref[...])

    if final_norm:
        @pl.when(j == pl.num_programs(1) - 1)
        def _():
            _rms_rows_to(o_ref, fg_ref, o_ref, min(128, o_ref.shape[0]))


def _ffn(x, g, w_in, w_out, final_g, layer, *, final_norm, tm=1024, tf=512):
    m, d = x.shape
    f = w_out.shape[1]
    tm = min(tm, m)
    nf = f // tf
    return pl.pallas_call(
        functools.partial(_ffn_kernel, final_norm=final_norm),
        grid=(m // tm, nf),
        in_specs=[pl.BlockSpec((tm, d), lambda i, j: (i, 0)),
                  pl.BlockSpec((1, d), lambda i, j: (0, 0)),
                  pl.BlockSpec((None, d, tf), lambda i, j: (layer, 0, j)),
                  pl.BlockSpec((None, d, tf), lambda i, j: (layer, 0, j + nf)),
                  pl.BlockSpec((None, tf, d), lambda i, j: (layer, j, 0)),
                  pl.BlockSpec((1, d), lambda i, j: (0, 0))],
        out_specs=pl.BlockSpec((tm, d), lambda i, j: (i, 0)),
        out_shape=jax.ShapeDtypeStruct((m, d), F32),
        scratch_shapes=[pltpu.VMEM((tm, d), BF16)],
        compiler_params=_cparams("parallel", "arbitrary"),
        name="ffn_final" if final_norm else "ffn",
    )(x, g.reshape(1, d), w_in, w_in, w_out, final_g.reshape(1, d))


def _branch_merge_kernel(ys_ref, yd_ref, ym_ref, g0_ref, g1_ref, g2_ref, ws_ref, wd_ref, wm_ref, o_ref):
    merged = (_sigmoid(g0_ref[...].astype(F32)) * _dot(ys_ref[...], ws_ref[...])
              + _sigmoid(g1_ref[...].astype(F32)) * _dot(yd_ref[...], wd_ref[...])
              + _sigmoid(g2_ref[...].astype(F32)) * _dot(ym_ref[...], wm_ref[...]))
    o_ref[...] = merged.astype(o_ref.dtype)


def _branch_merge(y_ssd, y_dsa, y_mem, proj, w_s, w_d, w_m, layer, *, tm=1024, tn=256):
    m = y_ssd.shape[0]
    d = D_MODEL
    tm = min(tm, m)
    gate_blk = P_GATE // tn
    per_branch = d // tn
    return pl.pallas_call(
        _branch_merge_kernel,
        grid=(m // tm, d // tn),
        in_specs=[pl.BlockSpec((tm, SSD_INNER), lambda i, j: (i, 0)),
                  pl.BlockSpec((tm, DSA_WIDTH), lambda i, j: (i, 0)),
                  pl.BlockSpec((tm, XA_WIDTH), lambda i, j: (i, 0)),
                  pl.BlockSpec((tm, tn), lambda i, j: (i, gate_blk + j)),
                  pl.BlockSpec((tm, tn), lambda i, j: (i, gate_blk + per_branch + j)),
                  pl.BlockSpec((tm, tn), lambda i, j: (i, gate_blk + 2 * per_branch + j)),
                  pl.BlockSpec((None, SSD_INNER, tn), lambda i, j: (layer, 0, j)),
                  pl.BlockSpec((None, DSA_WIDTH, tn), lambda i, j: (layer, 0, j)),
                  pl.BlockSpec((None, XA_WIDTH, tn), lambda i, j: (layer, 0, j))],
        out_specs=pl.BlockSpec((tm, tn), lambda i, j: (i, j)),
        out_shape=jax.ShapeDtypeStruct((m, d), BF16),
        compiler_params=_cparams("parallel", "parallel"),
        name="branch_merge",
    )(y_ssd, y_dsa, y_mem, proj, proj, proj, w_s, w_d, w_m)


def _out_proj_kernel(x_ref, m_ref, w_ref, o_ref):
    o_ref[...] = x_ref[...] + _dot(m_ref[...], w_ref[...])


def _out_proj(x, merged, w_o, layer, *, tm=512):
    m, d = x.shape
    tm = min(tm, m)
    return pl.pallas_call(
        _out_proj_kernel,
        grid=(m // tm,),
        in_specs=[pl.BlockSpec((tm, d), lambda i: (i, 0)),
                  pl.BlockSpec((tm, d), lambda i: (i, 0)),
                  pl.BlockSpec((None, d, d), lambda i: (layer, 0, 0))],
        out_specs=pl.BlockSpec((tm, d), lambda i: (i, 0)),
        out_shape=jax.ShapeDtypeStruct((m, d), F32),
        compiler_params=_cparams("parallel"),
        name="out_proj",
    )(x, merged, w_o)


def _memattn_kernel(q_ref, k_ref, v_ref, o_ref, *, rows):
    k = k_ref[...]
    v = v_ref[...]
    scale = XA_HEAD_DIM ** -0.5

    for r in range(q_ref.shape[0] // rows):
        sl = slice(r * rows, (r + 1) * rows)
        logits = _dot_nt(q_ref[sl, :], k) * scale
        mx = jnp.max(logits, axis=-1, keepdims=True)
        e = jnp.exp(logits - mx)
        p = e * (1.0 / jnp.sum(e, axis=-1, keepdims=True))
        o_ref[sl, :] = _dot(p.astype(BF16), v).astype(o_ref.dtype)


def _memattn(proj, kv, *, bsz, seq):
    qblk = P_QMEM // XA_HEAD_DIM
    return pl.pallas_call(
        functools.partial(_memattn_kernel, rows=min(1024, seq)),
        grid=(bsz, XA_HEADS),
        in_specs=[pl.BlockSpec((seq, XA_HEAD_DIM), lambda b, h: (b, qblk + h)),
                  pl.BlockSpec((MEM_LEN, XA_HEAD_DIM), lambda b, h: (b, h)),
                  pl.BlockSpec((MEM_LEN, XA_HEAD_DIM), lambda b, h: (b, XA_HEADS + h))],
        out_specs=pl.BlockSpec((seq, XA_HEAD_DIM), lambda b, h: (b, h)),
        out_shape=jax.ShapeDtypeStruct((bsz * seq, XA_WIDTH), BF16),
        compiler_params=_cparams("parallel", "parallel"),
        name="memattn",
    )(proj, kv, kv)


def _rope_tables_kernel(pos_ref, ca_ref, sa_ref, ci_ref, si_ref, *, rows):
    lane = lax.broadcasted_iota(jnp.int32, (1, LANES), 1)
    ln_theta = math.log(ROPE_THETA)
    inv_a = jnp.exp((lane & (DSA_ROPE // 2 - 1)).astype(F32) * (-2.0 * ln_theta / DSA_ROPE))
    inv_i = jnp.exp((lane & (IDX_ROPE // 2 - 1)).astype(F32) * (-2.0 * ln_theta / IDX_ROPE))
    lane_i = lane & (IDX_DIM - 1)

    def body(r, carry):
        sl = pl.ds(pl.multiple_of(r * rows, rows), rows)
        pos = pos_ref[sl, :].astype(F32)
        ang = pos * inv_a
        c, s = jnp.cos(ang), jnp.sin(ang)
        ca_ref[sl, :] = jnp.where(lane < DSA_ROPE, c, 1.0)
        sa_ref[sl, :] = jnp.where(lane < DSA_ROPE // 2, -s, jnp.where(lane < DSA_ROPE, s, 0.0))
        ang = pos * inv_i
        c, s = jnp.cos(ang), jnp.sin(ang)
        ci_ref[sl, :] = jnp.where(lane_i < IDX_ROPE, c, 1.0)
        si_ref[sl, :] = jnp.where(lane_i < IDX_ROPE // 2, -s, jnp.where(lane_i < IDX_ROPE, s, 0.0))
        return carry

    lax.fori_loop(0, pos_ref.shape[0] // rows, body, 0)


def _rope_tables(positions):
    bsz, seq = positions.shape
    tab = jax.ShapeDtypeStruct((bsz * seq, LANES), F32)
    spec = pl.BlockSpec((seq, LANES), lambda b: (b, 0))
    return pl.pallas_call(
        functools.partial(_rope_tables_kernel, rows=64),
        grid=(bsz,),
        in_specs=[pl.BlockSpec((seq, 1), lambda b: (b, 0))],
        out_specs=[spec, spec, spec, spec],
        out_shape=[tab, tab, tab, tab],
        compiler_params=_cparams("parallel"),
        name="rope_tables",
    )(positions.reshape(bsz * seq, 1))


def _swap_attn(x, lane):
    h = DSA_ROPE // 2
    return jnp.where(lane < h, pltpu.roll(x, LANES - h, 1), pltpu.roll(x, h, 1))


def _swap_idx(x, lane_i):
    h = IDX_ROPE // 2
    return jnp.where(lane_i < h, pltpu.roll(x, LANES - h, 1), pltpu.roll(x, h, 1))


def _tree_sum(parts):
    while len(parts) > 1:
        parts = [parts[a] + parts[a + 1] for a in range(0, len(parts) - 1, 2)] + ([parts[-1]] if len(parts) % 2 else [])
    return parts[0]


def _dsa_kernel(q_ref, qi_ref, sm_ref, ca_ref, sa_ref, ci_ref, si_ref, rota_ref, roti_ref, o_ref,
                k_ref, kil_ref, kih_ref, v_ref, xr_ref, key_ref, lim_ref, bias_ref, qs_ref, l_ref,
                m_ref, s_ref, acc_ref, *, topk, tq, kc):
    qb = pl.program_id(1)
    seq = sm_ref.shape[0]
    lane = lax.broadcasted_iota(jnp.int32, (1, LANES), 1)
    lane_i = lane & (IDX_DIM - 1)

    @pl.when(qb == 0)
    def _prep():
        rows = min(256, seq)

        def body(r, carry):
            sl = pl.ds(pl.multiple_of(r * rows, rows), rows)
            k = sm_ref[sl, S_K:S_K + LANES]
            k_ref[sl, :] = (k * ca_ref[sl, :] + _swap_attn(k, lane) * sa_ref[sl, :]).astype(BF16)
            v_ref[sl, :] = sm_ref[sl, S_V:S_V + LANES].astype(BF16)
            ki = sm_ref[sl, S_KI:S_KI + LANES]
            kir = ki * ci_ref[sl, :] + _swap_idx(ki, lane_i) * si_ref[sl, :]
            kil = jnp.where(lane < IDX_DIM, kir, 0.0)
            kil_ref[sl, :] = kil.astype(BF16)
            kih_ref[sl, :] = pltpu.roll(kil, IDX_DIM, 1).astype(BF16)
            return carry

        lax.fori_loop(0, seq // rows, body, 0)

    row0 = pl.multiple_of(qb * tq, tq)
    qrows = pl.ds(row0, tq)
    ca_q, sa_q = ca_ref[qrows, :], sa_ref[qrows, :]
    ci_q, si_q = ci_ref[qrows, :], si_ref[qrows, :]

    n_chunks = (row0 + tq + kc - 1) // kc

    def chunk_rows(c):
        return pl.ds(pl.multiple_of(c * kc, kc), kc)

    w_t = sm_ref[qrows, S_WI:S_WI + LANES].T * (IDX_DIM ** -0.5 * IDX_HEADS ** -0.5)

    def rotate(x, rot_ref, cos, sin):
        both = _dot(x, rot_ref[...])
        return (both[:, 0:LANES] * cos + both[:, LANES:2 * LANES] * sin).astype(BF16)

    for pr in range(IDX_HEADS // 2):
        xr_ref[pr] = rotate(qi_ref[:, pr * LANES:(pr + 1) * LANES], roti_ref, ci_q, si_q)
    qpos = row0 + lax.broadcasted_iota(jnp.int32, (kc, tq), 1)
    kpos0 = lax.broadcasted_iota(jnp.int32, (kc, tq), 0)

    def index_chunk(c, carry):
        ks = chunk_rows(c)
        kil, kih = kil_ref[ks, :], kih_ref[ks, :]
        score = jnp.zeros((kc, tq), F32)
        for pr in range(IDX_HEADS // 2):
            xr = xr_ref[pr]
            score = score + (jnp.maximum(_dot_nt(kil, xr), 0.0) * w_t[2 * pr:2 * pr + 1, :]
                             + jnp.maximum(_dot_nt(kih, xr), 0.0) * w_t[2 * pr + 1:2 * pr + 2, :])
        bits = lax.bitcast_convert_type(score, jnp.int32)
        skey = jnp.where(bits >= 0, bits, bits ^ jnp.int32(0x7FFFFFFF))
        key_ref[ks, :] = jnp.where(kpos0 + c * kc <= qpos, skey, jnp.int32(INT_MIN))
        return carry

    lax.fori_loop(0, n_chunks, index_chunk, 0)

    def count_keys(pred, also=None):
        def count_chunk(c, acc):
            k, pos = key_ref[chunk_rows(c), :], kpos0 + c * kc
            hit = jnp.where(pred(k, pos), 1.0, 0.0)
            if also is not None:
                hit = jnp.where(also(k, pos), hit, 0.0)
            return acc + _tree_sum([hit[j * 8:(j + 1) * 8, :] for j in range(kc // 8)])

        acc = lax.fori_loop(0, n_chunks, count_chunk, jnp.zeros((8, tq), F32))
        return jnp.sum(acc, axis=0, keepdims=True)

    def search(i, thr):
        cand = thr ^ lax.shift_left(jnp.int32(1), 31 - i)
        return jnp.where(count_keys(lambda k, pos: k >= cand) >= float(topk), cand, thr)

    thr = lax.fori_loop(0, 32, search, jnp.full((1, tq), INT_MIN, jnp.int32))

    need = float(topk) - count_keys(lambda k, pos: k > thr)
    n_tied = count_keys(lambda k, pos: k == thr)
    lim_ref[...] = jnp.full(lim_ref.shape, seq, jnp.int32)
    surplus = jnp.max(jnp.where(thr > INT_MIN, n_tied - need, 0.0))

    @pl.when(surplus > 0.0)
    def _():
        nbits = (seq - 1).bit_length()

        def refine(i, lo):
            cand = lo + lax.shift_left(jnp.int32(1), nbits - 1 - i)
            below = count_keys(lambda k, pos: k == thr, lambda k, pos: pos < cand)
            return jnp.where(below < need, cand, lo)

        lo = lax.fori_loop(0, nbits, refine, jnp.zeros((1, tq), jnp.int32))
        lim_ref[...] = jnp.broadcast_to(lo + 1, lim_ref.shape)

    def bias_chunk(c, carry):
        k = key_ref[chunk_rows(c), :]
        pos = kpos0 + c * kc
        visible = jnp.where(pos <= qpos, 0.0, -jnp.inf)
        tied = jnp.where(k == thr, jnp.where(pos < lim_ref[0:1, :], visible, -jnp.inf), -jnp.inf)
        bias = jnp.where(k > thr, visible, tied)
        for kb in range(kc // LANES):
            bias_ref[c, :, kb * LANES:(kb + 1) * LANES] = bias[kb * LANES:(kb + 1) * LANES, :].T
        return carry

    lax.fori_loop(0, n_chunks, bias_chunk, 0)

    for h in range(DSA_HEADS):
        qs_ref[h * tq:(h + 1) * tq, :] = rotate(q_ref[:, h * LANES:(h + 1) * LANES], rota_ref, ca_q, sa_q)

    scale2 = DSA_HEAD_DIM ** -0.5 * math.log2(math.e)

    def logits_chunk(c, first):
        lg = _dot_nt(qs_ref[...], k_ref[chunk_rows(c), :]) * scale2
        lg = lg + jnp.concatenate([bias_ref[c]] * DSA_HEADS, axis=0)
        l_ref[c] = lg
        part = lg[:, 0:LANES]
        for kb in range(1, kc // LANES):
            part = jnp.maximum(part, lg[:, kb * LANES:(kb + 1) * LANES])
        m_ref[...] = part if first else jnp.maximum(m_ref[...], part)

    logits_chunk(0, True)
    lax.fori_loop(1, n_chunks, lambda c, carry: (logits_chunk(c, False), carry)[1], 0)
    m_ref[...] = jnp.broadcast_to(jnp.max(m_ref[...], axis=-1, keepdims=True), m_ref.shape)

    def value_chunk(c, first):
        e = jnp.exp2(l_ref[c] - jnp.concatenate([m_ref[...]] * (kc // LANES), axis=1))
        rowsum = _tree_sum([e[:, kb * LANES:(kb + 1) * LANES] for kb in range(kc // LANES)])
        out = _dot(e.astype(BF16), v_ref[chunk_rows(c), :])
        s_ref[...] = rowsum if first else s_ref[...] + rowsum
        acc_ref[...] = out if first else acc_ref[...] + out

    value_chunk(0, True)
    lax.fori_loop(1, n_chunks, lambda c, carry: (value_chunk(c, False), carry)[1], 0)
    for h in range(DSA_HEADS):
        hs = slice(h * tq, (h + 1) * tq)
        inv = 1.0 / jnp.sum(s_ref[hs, :], axis=-1, keepdims=True)
        o_ref[:, h * LANES:(h + 1) * LANES] = (acc_ref[hs, :] * inv).astype(o_ref.dtype)


def _dsa(proj, narrow, tables, *, bsz, seq):
    tq = DSA_QBLOCK
    kc = min(256, seq)
    topk = min(DSA_TOPK_MAX, seq // 4)
    nq = seq // tq
    tab_spec = pl.BlockSpec((seq, LANES), lambda b, i: (b, 0))
    rot_spec = pl.BlockSpec((LANES, 2 * LANES), lambda b, i: (0, 0))

    def partner_matrix(period, half):
        p = np.zeros((LANES, LANES), np.float32)
        j = np.arange(LANES)
        first, second = (j % period) < half, ((j % period) >= half) & ((j % period) < 2 * half)
        p[j[first] + half, j[first]] = 1.0
        p[j[second] - half, j[second]] = 1.0
        return jnp.asarray(np.concatenate([np.eye(LANES, dtype=np.float32), p], axis=1), BF16)

    rot_a = partner_matrix(LANES, DSA_ROPE // 2)
    rot_i = partner_matrix(IDX_DIM, IDX_ROPE // 2)
    return pl.pallas_call(
        functools.partial(_dsa_kernel, topk=topk, tq=tq, kc=kc),
        grid=(bsz, nq),
        in_specs=[pl.BlockSpec((tq, DSA_WIDTH), lambda b, i: (b * nq + i, P_Q // DSA_WIDTH)),
                  pl.BlockSpec((tq, IDX_HEADS * IDX_DIM), lambda b, i: (b * nq + i, P_QI // (IDX_HEADS * IDX_DIM))),
                  pl.BlockSpec((seq, S_WIDTH), lambda b, i: (b, 0)),
                  tab_spec, tab_spec, tab_spec, tab_spec, rot_spec, rot_spec],
        out_specs=pl.BlockSpec((tq, DSA_WIDTH), lambda b, i: (b * nq + i, 0)),
        out_shape=jax.ShapeDtypeStruct((bsz * seq, DSA_WIDTH), BF16),
        scratch_shapes=[pltpu.VMEM((seq, LANES), BF16),
                        pltpu.VMEM((seq, LANES), BF16),
                        pltpu.VMEM((seq, LANES), BF16),
                        pltpu.VMEM((seq, LANES), BF16),
                        pltpu.VMEM((IDX_HEADS // 2, tq, LANES), BF16),
                        pltpu.VMEM((seq, tq), jnp.int32),
                        pltpu.VMEM((8, tq), jnp.int32),
                        pltpu.VMEM((seq // kc, tq, kc), F32),
                        pltpu.VMEM((DSA_HEADS * tq, LANES), BF16),
                        pltpu.VMEM((seq // kc, DSA_HEADS * tq, kc), F32),
                        pltpu.VMEM((DSA_HEADS * tq, LANES), F32),
                        pltpu.VMEM((DSA_HEADS * tq, LANES), F32),
                        pltpu.VMEM((DSA_HEADS * tq, LANES), F32)],
        compiler_params=_cparams("parallel", "arbitrary"),
        name="dsa",
    )(proj, proj, narrow, *tables, rot_a, rot_i)


def _split_bf16(x, parts):
    out = []
    for _ in range(parts - 1):
        hi = x.astype(BF16)
        out.append(hi)
        x = x - hi.astype(F32)
    out.append(x.astype(BF16))
    return out


def _expand_heads(x, e_ref, parts):
    pieces = [p.astype(F32) for p in _split_bf16(x, parts)]
    k = parts * SSD_HEADS
    k_pad = -k % LANES
    if k_pad:
        pieces.append(jnp.zeros((x.shape[0], k_pad), F32))
    lhs = jnp.concatenate(pieces, axis=1).astype(BF16)
    return _dot(lhs, e_ref[0:k + k_pad, :])


def _ssd_kernel(z_ref, xs_ref, bc_ref, sm_ref, cwx_ref, cwb_ref, cbx_ref, cbb_ref,
                dtb_ref, alog_ref, dskip_ref, ng_ref, e_ref, shift_ref, o_ref,
                xcat_ref, bcat_ref, state_ref, y_ref):
    c = pl.program_id(1)
    q = SSD_CHUNK
    tail = SSD_TAIL

    @pl.when(c == 0)
    def _():
        xcat_ref[q:, :] = jnp.zeros((xcat_ref.shape[0] - q, SSD_INNER), xcat_ref.dtype)
        bcat_ref[q:, :] = jnp.zeros((bcat_ref.shape[0] - q, SSD_BC), bcat_ref.dtype)
        state_ref[...] = jnp.zeros_like(state_ref)

    xcat_ref[0:q, :] = xs_ref[...]
    bcat_ref[0:q, :] = bc_ref[...]

    def conv_silu(cat_ref, w_ref, b_ref):
        width = cat_ref.shape[1]
        delayed = _dot(shift_ref[...], cat_ref[...]).reshape(q // 8, SSD_CONV, 8, width)
        acc = b_ref[...] + w_ref[SSD_CONV - 1:SSD_CONV, :] * delayed[:, 0]
        for j in range(1, SSD_CONV):
            acc = acc + w_ref[SSD_CONV - 1 - j:SSD_CONV - j, :] * delayed[:, j]
        acc = acc.reshape(q, width)
        return acc * _sigmoid(acc)

    xs = conv_silu(xcat_ref, cwx_ref, cbx_ref)
    bc = conv_silu(bcat_ref, cwb_ref, cbb_ref)
    xcat_ref[q:q + tail, :] = xcat_ref[q - tail:q, :]
    bcat_ref[q:q + tail, :] = bcat_ref[q - tail:q, :]

    pre = sm_ref[:, S_DT:S_DT + SSD_HEADS] + dtb_ref[...]
    dt = jnp.maximum(pre, 0.0) + jnp.log1p(jnp.exp(-jnp.abs(pre)))
    da = dt * (-jnp.exp(alog_ref[...]))
    li = lax.broadcasted_iota(jnp.int32, (q, q), 0)
    si = lax.broadcasted_iota(jnp.int32, (q, q), 1)
    causal = li >= si
    tri = jnp.where(causal, 1.0, 0.0).astype(BF16)
    a_cs = sum(_dot(tri, part) for part in _split_bf16(da, 3))
    a_cs_t = jnp.concatenate([a_cs, jnp.zeros_like(a_cs)], axis=1).T

    dt_x = _expand_heads(dt, e_ref, 2)
    acs_x = _expand_heads(a_cs, e_ref, 3)
    ea_x = jnp.exp(acs_x)
    dte_x = jnp.exp(acs_x[q - 1:q, :] - acs_x)
    xdt = xs * dt_x
    xdt_b = xdt.astype(BF16)
    xw_b = (xdt * dte_x).astype(BF16)
    cd_x = ea_x[q - 1:q, :]

    gw = SSD_GROUP_WIDTH
    for g in range(SSD_GROUPS):
        bm = bc[:, g * SSD_STATE:(g + 1) * SSD_STATE]
        cm_b = bc[:, (SSD_GROUPS + g) * SSD_STATE:(SSD_GROUPS + g + 1) * SSD_STATE].astype(BF16)
        cb = _dot_nt(cm_b, bm.astype(BF16))
        st = state_ref[g]
        y_ref[:, g * gw:(g + 1) * gw] = _dot(cm_b, st.astype(BF16)) * ea_x[:, g * gw:(g + 1) * gw]
        for r in range(SSD_HEADS_PER_GROUP):
            h = g * SSD_HEADS_PER_GROUP + r
            seg = a_cs[:, h:h + 1] - a_cs_t[h:h + 1, :]
            decay = jnp.exp(jnp.where(causal, seg, -jnp.inf))
            lo, hi = h * SSD_HEAD_DIM, (h + 1) * SSD_HEAD_DIM
            y_ref[:, lo:hi] += _dot((cb * decay).astype(BF16), xdt_b[:, lo:hi])
        upd = _dot(bm.T.astype(BF16), xw_b[:, g * gw:(g + 1) * gw])
        state_ref[g] = st * cd_x[:, g * gw:(g + 1) * gw] + upd

    z = z_ref[...].astype(F32)
    y = (y_ref[...] + xs * dskip_ref[...]) * (z * _sigmoid(z))
    ms = jnp.mean(y * y, axis=-1, keepdims=True)
    o_ref[...] = (y * lax.rsqrt(ms + NORM_EPS) * ng_ref[...]).astype(o_ref.dtype)


def _ssd(proj, narrow, conv_w, conv_b, dt_bias, a_log, d_skip, norm_g, *, bsz, seq):
    q = SSD_CHUNK
    nc = seq // q
    one_hot = np.repeat(np.eye(SSD_HEADS, dtype=np.float32), SSD_HEAD_DIM, axis=1)
    expand = jnp.asarray(np.concatenate([one_hot] * 3 + [np.zeros_like(one_hot)], axis=0), BF16)
    shift = np.zeros((SSD_CONV * q, SSD_STAGE_ROWS), np.float32)
    t = np.arange(q)
    for j in range(SSD_CONV):
        shift[(t // 8) * (8 * SSD_CONV) + j * 8 + t % 8, np.where(t >= j, t - j, q + SSD_TAIL + t - j)] = 1.0
    shift = jnp.asarray(shift, BF16)
    row = lambda b, c: (b * nc + c)
    const = lambda b, c: (0, 0)
    return pl.pallas_call(
        _ssd_kernel,
        grid=(bsz, nc),
        in_specs=[pl.BlockSpec((q, SSD_INNER), lambda b, c: (row(b, c), P_Z // SSD_INNER)),
                  pl.BlockSpec((q, SSD_INNER), lambda b, c: (row(b, c), P_XS // SSD_INNER)),
                  pl.BlockSpec((q, SSD_BC), lambda b, c: (row(b, c), P_BC // SSD_BC)),
                  pl.BlockSpec((q, S_WIDTH), lambda b, c: (row(b, c), 0)),
                  pl.BlockSpec((SSD_CONV, SSD_INNER), const),
                  pl.BlockSpec((SSD_CONV, SSD_BC), const),
                  pl.BlockSpec((1, SSD_INNER), const),
                  pl.BlockSpec((1, SSD_BC), const),
                  pl.BlockSpec((1, SSD_HEADS), const),
                  pl.BlockSpec((1, SSD_HEADS), const),
                  pl.BlockSpec((1, SSD_INNER), const),
                  pl.BlockSpec((1, SSD_INNER), const),
                  pl.BlockSpec((4 * SSD_HEADS, SSD_INNER), const),
                  pl.BlockSpec((SSD_CONV * q, SSD_STAGE_ROWS), const)],
        out_specs=pl.BlockSpec((q, SSD_INNER), lambda b, c: (row(b, c), 0)),
        out_shape=jax.ShapeDtypeStruct((bsz * seq, SSD_INNER), BF16),
        scratch_shapes=[pltpu.VMEM((SSD_STAGE_ROWS, SSD_INNER), BF16),
                        pltpu.VMEM((SSD_STAGE_ROWS, SSD_BC), BF16),
                        pltpu.VMEM((SSD_GROUPS, SSD_STATE, SSD_GROUP_WIDTH), F32),
                        pltpu.VMEM((q, SSD_INNER), F32)],
        compiler_params=_cparams("parallel", "arbitrary"),
        name="ssd",
    )(proj, proj, proj, narrow,
      conv_w[:, :SSD_INNER], conv_w[:, SSD_INNER:], conv_b[None, :SSD_INNER], conv_b[None, SSD_INNER:],
      dt_bias[None, :], a_log[None, :], jnp.repeat(d_skip, SSD_HEAD_DIM)[None, :], norm_g[None, :], expand, shift)


def _regroup_kernel(w_ref, wide_ref, narrow_ref, *, step):
    def move(dst_ref, src, dst, width):
        for c in range(0, width, step):
            n = min(step, width - c)
            dst_ref[dst + c:dst + c + n, :] = w_ref[src + c:src + c + n, :].astype(dst_ref.dtype)

    for src, dst, width in _WIDE_PIECES:
        move(wide_ref, src, dst, width)
    used = S_WI + IDX_HEADS
    narrow_ref[used:, :] = jnp.zeros((S_WIDTH - used, narrow_ref.shape[1]), narrow_ref.dtype)
    for src, dst, width in _NARROW_PIECES:
        move(narrow_ref, src, dst, width)


def _regroup_w_in(w_in_t, *, cols=128):
    depth, n, d = w_in_t.shape
    return pl.pallas_call(
        functools.partial(_regroup_kernel, step=1024),
        grid=(depth, d // cols),
        in_specs=[pl.BlockSpec((None, n, cols), lambda l, c: (l, 0, c))],
        out_specs=[pl.BlockSpec((None, P_WIDTH, cols), lambda l, c: (l, 0, c)),
                   pl.BlockSpec((None, S_WIDTH, cols), lambda l, c: (l, 0, c))],
        out_shape=[jax.ShapeDtypeStruct((depth, P_WIDTH, d), BF16),
                   jax.ShapeDtypeStruct((depth, S_WIDTH, d), BF16)],
        compiler_params=_cparams("parallel", "parallel"),
        name="regroup_w_in",
    )(w_in_t)


def kernel(x, mem, positions, ffn1_norm, w_ffn1_in, w_ffn1_out, mix_norm, w_in, conv_w, conv_b, dt_bias, a_log,
           d_skip, ssd_norm, mem_norm, w_mem_kv, w_br_ssd, w_br_dsa, w_br_mem, w_out, ffn2_norm, w_ffn2_in,
           w_ffn2_out, final_norm):
    bsz, seq, d = x.shape
    depth = w_in.shape[0]
    assert d == D_MODEL and seq % DSA_QBLOCK == 0 and seq % SSD_CHUNK == 0 and mem.shape[1] == MEM_LEN

    w_p, w_n = _regroup_w_in(jnp.swapaxes(w_in, 1, 2))
    w1i, w1o = w_ffn1_in.astype(BF16), w_ffn1_out.astype(BF16)
    w2i, w2o = w_ffn2_in.astype(BF16), w_ffn2_out.astype(BF16)
    wkv = w_mem_kv.astype(BF16)
    wbs, wbd, wbm, wo = (w.astype(BF16) for w in (w_br_ssd, w_br_dsa, w_br_mem, w_out))

    tables = _rope_tables(positions)
    xf = x.reshape(bsz * seq, d)
    memf = mem.reshape(bsz * MEM_LEN, d)
    for l in range(depth):
        xf = _ffn(xf, ffn1_norm[l], w1i, w1o, final_norm, l, final_norm=False)
        proj, narrow = _in_proj(xf, mix_norm[l], w_p, w_n, l)
        y_ssd = _ssd(proj, narrow, conv_w[l], conv_b[l], dt_bias[l], a_log[l], d_skip[l], ssd_norm[l],
                     bsz=bsz, seq=seq)
        y_dsa = _dsa(proj, narrow, tables, bsz=bsz, seq=seq)
        kv = _norm_matmul(memf, mem_norm[l], wkv, l, tm=1024, tn=512, out_dtype=BF16, name="mem_kv")
        y_mem = _memattn(proj, kv, bsz=bsz, seq=seq)
        merged = _branch_merge(y_ssd, y_dsa, y_mem, proj, wbs, wbd, wbm, l)
        xf = _out_proj(xf, merged, wo, l)
        xf = _ffn(xf, ffn2_norm[l], w2i, w2o, final_norm, l, final_norm=(l == depth - 1))
    return xf.reshape(bsz, seq, d)
```
